```python
import math
import jax
import jax.numpy as jnp
from jax import lax
import numpy as np

D_MODEL = 4096
BATCH = 1
SEQ = 16384
DEPTH = 2
DEC_BATCH = 2
DEC_SEQ = 4096
PAST_LEN = 128

N_BRANCH = 4
MIX_WIDTH = 1024

DA_HEADS = 8
DA_HEAD_DIM = 64
DA_V_DIM = 2 * DA_HEAD_DIM
DA_QK_WIDTH = DA_HEADS * 2 * DA_HEAD_DIM
ROPE_THETA = 500000.0
ROPE_DIM = DA_HEAD_DIM // 4
Q_BLOCK = 128
FN_GROUPS = 4
FN_GROUP_DIM = MIX_WIDTH // FN_GROUPS
CONV_TAPS = 31
GLA_HEADS = 4
GLA_DK = 128
GLA_DV = 256
GLA_K_WIDTH = GLA_HEADS * GLA_DK
GLA_RANK = 16
GLA_TAU = 16.0
GLA_CHUNK = 64

COL_Q = 0
COL_K = COL_Q + DA_QK_WIDTH
COL_V = COL_K + DA_QK_WIDTH
COL_FN = COL_V + MIX_WIDTH
COL_CV = COL_FN + MIX_WIDTH
COL_GQ = COL_CV + 2 * MIX_WIDTH
COL_GK = COL_GQ + GLA_K_WIDTH
COL_GV = COL_GK + GLA_K_WIDTH
COL_GR = COL_GV + MIX_WIDTH
COL_GA = COL_GR + MIX_WIDTH
IN_WIDTH = COL_GA + 2 * GLA_RANK

D_FF = 11008
N_EXPERTS = 8
TOP_K = 2
D_FF_EXPERT = 1792
N_DENSE = (DEPTH + 1) // 2
N_MOE = DEPTH // 2

RMS_EPS = 1e-6
LN_EPS = 1e-5

kernel_name = 'hybrid_gated_parallel_encoder'


def rms_norm(x, g):
    xf = x.astype(jnp.float32)
    y = xf * lax.rsqrt(jnp.mean(xf * xf, axis=-1, keepdims=True) + RMS_EPS)
    return (y * g.astype(jnp.float32)).astype(x.dtype)


def layer_norm(x, g, b):
    xf = x.astype(jnp.float32)
    mu = jnp.mean(xf, axis=-1, keepdims=True)
    xc = xf - mu
    y = xc * lax.rsqrt(jnp.mean(xc * xc, axis=-1, keepdims=True) + LN_EPS)
    return (y * g.astype(jnp.float32) + b.astype(jnp.float32)).astype(x.dtype)


def rope_partial(t, pos):
    half = ROPE_DIM // 2
    inv_freq = 1.0 / (ROPE_THETA ** (jnp.arange(half, dtype=jnp.float32) * 2.0 / ROPE_DIM))
    ang = pos[:, None] * inv_freq[None, :]
    cos = jnp.cos(ang)[None, :, None, None, :]
    sin = jnp.sin(ang)[None, :, None, None, :]
    tr = t[..., :ROPE_DIM].astype(jnp.float32)
    t1, t2 = tr[..., :half], tr[..., half:]
    rot = jnp.concatenate([t1 * cos - t2 * sin, t2 * cos + t1 * sin], axis=-1).astype(t.dtype)
    return jnp.concatenate([rot, t[..., ROPE_DIM:]], axis=-1)


def diff_attention(q, k, v, lam, lam_init, g_subln):
    B, L = q.shape[0], q.shape[1]
    nb = L // Q_BLOCK
    scale = DA_HEAD_DIM ** -0.5
    qb = jnp.moveaxis(q.reshape(B, nb, Q_BLOCK, DA_HEADS, 2, DA_HEAD_DIM), 1, 0)

    def block(qi):
        s = jnp.einsum('bqhcd,bkhcd->bhcqk', qi, k, preferred_element_type=jnp.float32) * scale
        p = jax.nn.softmax(s, axis=-1)
        a = p[:, :, 0] - lam * p[:, :, 1]
        return jnp.einsum('bhqk,bkhv->bqhv', a.astype(v.dtype), v)

    o = lax.map(block, qb)
    o = jnp.moveaxis(o, 0, 1).reshape(B, L, DA_HEADS, DA_V_DIM)
    o = rms_norm(o, g_subln) * (1.0 - lam_init)
    return o.reshape(B, L, MIX_WIDTH)


def fourier_mix(u):
    B, L = u.shape[0], u.shape[1]
    uf = u.astype(jnp.float32).reshape(B, L, FN_GROUPS, FN_GROUP_DIM).transpose(0, 2, 1, 3)
    f = jnp.fft.fft2(uf, norm='ortho').real
    return f.transpose(0, 2, 1, 3).reshape(B, L, MIX_WIDTH).astype(u.dtype)


def conformer_conv(u, w_dw, b_dw, ln_g, ln_b):
    a, gt = jnp.split(u, 2, axis=-1)
    z = a * jax.nn.sigmoid(gt)
    z = lax.conv_general_dilated(
        z, w_dw[:, None, :], window_strides=(1,),
        padding=[(CONV_TAPS // 2, CONV_TAPS // 2)],
        dimension_numbers=('NWC', 'WIO', 'NWC'),
        feature_group_count=MIX_WIDTH) + b_dw
    z = layer_norm(z, ln_g, ln_b)
    return jax.nn.silu(z)


def gla_scan_dir(q, k, v, g):
    B, H, L, Dk = q.shape
    Dv = v.shape[-1]
    n = L // GLA_CHUNK

    def to_chunks(t):
        return jnp.moveaxis(t.reshape(B, H, n, GLA_CHUNK, t.shape[-1]), 2, 0)

    mask = jnp.tril(jnp.ones((GLA_CHUNK, GLA_CHUNK), dtype=bool))

    def step(S, inp):
        qi, ki, vi, gi = inp
        b = jnp.cumsum(gi, axis=2)
        o_inter = jnp.einsum('bhtk,bhkv->bhtv', qi * jnp.exp(b), S)
        diff = b[:, :, :, None, :] - b[:, :, None, :, :]
        decay = jnp.exp(jnp.where(mask[:, :, None], diff, -jnp.inf))
        A = jnp.einsum('bhtk,bhsk,bhtsk->bhts', qi, ki, decay)
        o_intra = jnp.einsum('bhts,bhsv->bhtv', A, vi)
        b_last = b[:, :, -1, :]
        S_new = jnp.exp(b_last)[..., None] * S + jnp.einsum(
            'bhsk,bhsv->bhkv', ki * jnp.exp(b_last[:, :, None, :] - b), vi)
        return S_new, o_inter + o_intra

    S0 = jnp.zeros((B, H, Dk, Dv), jnp.float32)
    _, o = lax.scan(step, S0, (to_chunks(q), to_chunks(k), to_chunks(v), to_chunks(g)))
    return jnp.moveaxis(o, 0, 2).reshape(B, H, L, Dv)


def gla_mixer(gq, gk, gv, gr, ga, w_a2, b_a2, g_norm):
    B, L = gq.shape[0], gq.shape[1]

    def heads(t, d):
        return t.astype(jnp.float32).reshape(B, L, GLA_HEADS, d).transpose(0, 2, 1, 3)

    q = heads(gq, GLA_DK) * (GLA_DK ** -0.5)
    k = heads(gk, GLA_DK)
    v = heads(gv, GLA_DV)
    g_fwd = heads(jax.nn.log_sigmoid((ga[..., :GLA_RANK] @ w_a2[0] + b_a2[0]).astype(jnp.float32)) / GLA_TAU, GLA_DK)
    g_bwd = heads(jax.nn.log_sigmoid((ga[..., GLA_RANK:] @ w_a2[1] + b_a2[1]).astype(jnp.float32)) / GLA_TAU, GLA_DK)
    flip = lambda t: jnp.flip(t, axis=2)
    o = gla_scan_dir(q, k, v, g_fwd) + flip(gla_scan_dir(flip(q), flip(k), flip(v), flip(g_bwd)))
    o = rms_norm(o.transpose(0, 2, 1, 3), g_norm)
    r = jax.nn.silu(gr.astype(jnp.float32).reshape(B, L, GLA_HEADS, GLA_DV))
    return (o * r).reshape(B, L, MIX_WIDTH).astype(gq.dtype)


def swiglu(x, wg, wu, wd):
    return (jax.nn.silu(x @ wg) * (x @ wu)) @ wd


def moe_swiglu(x, w_router, wg, wu, wd):
    logits = jnp.einsum('bld,de->ble', x, w_router, preferred_element_type=jnp.float32)
    top_v, top_i = lax.top_k(logits, TOP_K)
    w = jax.nn.softmax(top_v, axis=-1)
    gates = jnp.sum(jax.nn.one_hot(top_i, N_EXPERTS, dtype=jnp.float32) * w[..., None], axis=-2)
    y = jnp.zeros_like(x)
    for e in range(N_EXPERTS):
        y = y + gates[..., e:e + 1].astype(x.dtype) * swiglu(x, wg[e], wu[e], wd[e])
    return y


def trunk(x, norm_mix, w_in, da_lambda, da_subln_g, cv_dw_w, cv_dw_b, cv_ln_g, cv_ln_b,
          gla_w_a2, gla_b_a2, gla_norm_g, w_gate, b_gate, w_branch, w_out, norm_ffn,
          ffn_w_gate, ffn_w_up, ffn_w_down, moe_router, moe_w_gate, moe_w_up, moe_w_down,
          norm_final):
    B, L = x.shape[0], x.shape[1]
    pos = jnp.arange(L, dtype=jnp.float32)
    for l in range(DEPTH):
        h = rms_norm(x, norm_mix[l])
        u = h @ w_in[l]
        q = rope_partial(u[..., COL_Q:COL_K].reshape(B, L, DA_HEADS, 2, DA_HEAD_DIM), pos)
        k = rope_partial(u[..., COL_K:COL_V].reshape(B, L, DA_HEADS, 2, DA_HEAD_DIM), pos)
        v = u[..., COL_V:COL_FN].reshape(B, L, DA_HEADS, DA_V_DIM)
        lam_init = 0.8 - 0.6 * math.exp(-0.3 * l)
        lp = da_lambda[l].astype(jnp.float32)
        lam = jnp.exp(jnp.sum(lp[0] * lp[1])) - jnp.exp(jnp.sum(lp[2] * lp[3])) + lam_init
        z_a = diff_attention(q, k, v, lam, lam_init, da_subln_g[l])
        z_b = fourier_mix(u[..., COL_FN:COL_CV])
        z_c = conformer_conv(u[..., COL_CV:COL_GQ], cv_dw_w[l], cv_dw_b[l], cv_ln_g[l], cv_ln_b[l])
        z_d = gla_mixer(u[..., COL_GQ:COL_GK], u[..., COL_GK:COL_GV], u[..., COL_GV:COL_GR],
                        u[..., COL_GR:COL_GA], u[..., COL_GA:IN_WIDTH],
                        gla_w_a2[l], gla_b_a2[l], gla_norm_g[l])
        merged = jnp.zeros_like(x)
        for i, z in enumerate((z_a, z_b, z_c, z_d)):
            gate = jax.nn.sigmoid(h @ w_gate[l, i] + b_gate[l, i])
            merged = merged + gate * (z @ w_branch[l, i])
        x = x + merged @ w_out[l]
        h2 = rms_norm(x, norm_ffn[l])
        if l % 2 == 0:
            j = l // 2
            x = x + swiglu(h2, ffn_w_gate[j], ffn_w_up[j], ffn_w_down[j])
        else:
            j = l // 2
            x = x + moe_swiglu(h2, moe_router[j], moe_w_gate[j], moe_w_up[j], moe_w_down[j])
    return rms_norm(x, norm_final)


def setup_inputs(seed: int = 0) -> dict:
    key = jax.random.key(seed)
    ks = jax.random.split(key, 28)
    f32 = jnp.float32

    def nrm(k, shape, scale):
        return jax.random.normal(k, shape, f32) * scale

    def gain(k, shape):
        return 1.0 + 0.02 * jax.random.normal(k, shape, f32)

    return {
        'x_prompt': nrm(ks[0], (BATCH, SEQ, D_MODEL), 1.0),
        'x_sample': nrm(ks[1], (DEC_BATCH, DEC_SEQ, D_MODEL), 1.0),
        'norm_mix': gain(ks[2], (DEPTH, D_MODEL)),
        'w_in': nrm(ks[3], (DEPTH, D_MODEL, IN_WIDTH), D_MODEL ** -0.5),
        'da_lambda': nrm(ks[4], (DEPTH, 4, DA_HEAD_DIM), 0.1),
        'da_subln_g': gain(ks[5], (DEPTH, DA_V_DIM)),
        'cv_dw_w': nrm(ks[6], (DEPTH, CONV_TAPS, MIX_WIDTH), CONV_TAPS ** -0.5),
        'cv_dw_b': nrm(ks[7], (DEPTH, MIX_WIDTH), 0.02),
        'cv_ln_g': gain(ks[8], (DEPTH, MIX_WIDTH)),
        'cv_ln_b': nrm(ks[9], (DEPTH, MIX_WIDTH), 0.02),
        'gla_w_a2': nrm(ks[10], (DEPTH, 2, GLA_RANK, GLA_K_WIDTH), GLA_RANK ** -0.5),
        'gla_b_a2': nrm(ks[11], (DEPTH, 2, GLA_K_WIDTH), 0.02),
        'gla_norm_g': gain(ks[12], (DEPTH, GLA_DV)),
        'w_gate': nrm(ks[13], (DEPTH, N_BRANCH, D_MODEL, D_MODEL), D_MODEL ** -0.5),
        'b_gate': nrm(ks[14], (DEPTH, N_BRANCH, D_MODEL), 0.02),
        'w_branch': nrm(ks[15], (DEPTH, N_BRANCH, MIX_WIDTH, D_MODEL), MIX_WIDTH ** -0.5),
        'w_out': nrm(ks[16], (DEPTH, D_MODEL, D_MODEL), D_MODEL ** -0.5),
        'norm_ffn': gain(ks[17], (DEPTH, D_MODEL)),
        'ffn_w_gate': nrm(ks[18], (N_DENSE, D_MODEL, D_FF), D_MODEL ** -0.5),
        'ffn_w_up': nrm(ks[19], (N_DENSE, D_MODEL, D_FF), D_MODEL ** -0.5),
        'ffn_w_down': nrm(ks[20], (N_DENSE, D_FF, D_MODEL), D_FF ** -0.5),
        'moe_router': nrm(ks[21], (N_MOE, D_MODEL, N_EXPERTS), D_MODEL ** -0.5),
        'moe_w_gate': nrm(ks[22], (N_MOE, N_EXPERTS, D_MODEL, D_FF_EXPERT), D_MODEL ** -0.5),
        'moe_w_up': nrm(ks[23], (N_MOE, N_EXPERTS, D_MODEL, D_FF_EXPERT), D_MODEL ** -0.5),
        'moe_w_down': nrm(ks[24], (N_MOE, N_EXPERTS, D_FF_EXPERT, D_MODEL), D_FF_EXPERT ** -0.5),
        'norm_final': gain(ks[25], (D_MODEL,)),
    }


def reference(x_prompt, x_sample, norm_mix, w_in, da_lambda, da_subln_g, cv_dw_w, cv_dw_b,
              cv_ln_g, cv_ln_b, gla_w_a2, gla_b_a2, gla_norm_g, w_gate, b_gate, w_branch,
              w_out, norm_ffn, ffn_w_gate, ffn_w_up, ffn_w_down, moe_router, moe_w_gate,
              moe_w_up, moe_w_down, norm_final):
    y_prompt = trunk(x_prompt, norm_mix, w_in, da_lambda, da_subln_g, cv_dw_w, cv_dw_b,
                     cv_ln_g, cv_ln_b, gla_w_a2, gla_b_a2, gla_norm_g, w_gate, b_gate,
                     w_branch, w_out, norm_ffn, ffn_w_gate, ffn_w_up, ffn_w_down,
                     moe_router, moe_w_gate, moe_w_up, moe_w_down, norm_final)
    y_sample = trunk(x_sample, norm_mix, w_in, da_lambda, da_subln_g, cv_dw_w, cv_dw_b,
                     cv_ln_g, cv_ln_b, gla_w_a2, gla_b_a2, gla_norm_g, w_gate, b_gate,
                     w_branch, w_out, norm_ffn, ffn_w_gate, ffn_w_up, ffn_w_down,
                     moe_router, moe_w_gate, moe_w_up, moe_w_down, norm_final)
    return (y_prompt, y_sample)
```

```python
import functools
import math

import jax
import jax.numpy as jnp
from jax import lax
from jax.experimental import pallas as pl
from jax.experimental.pallas import tpu as pltpu

F32 = jnp.float32
BF16 = jnp.bfloat16

MIX = 1024
DA_HEADS = 8
DA_HEAD_DIM = 64
DA_V_DIM = 128
ROPE_DIM = 16
ROPE_THETA = 500000.0
FN_GROUPS = 4
FN_GROUP_DIM = 256
CONV_TAPS = 31
CONV_HALO = 16
GLA_HEADS = 4
GLA_DK = 128
GLA_DV = 256
GLA_RANK = 16
GLA_TAU = 16.0
GLA_CHUNK = 64
TOP_K = 2
RMS_EPS = 1e-6
LN_EPS = 1e-5

COL_Q = 0
COL_K = 1024
COL_V = 2048
COL_FN = 3072
COL_CV = 4096
COL_GQ = 6144
COL_GK = 6656
COL_GV = 7168
COL_GR = 8192
COL_GA = 9216
LANE = 128

VMEM_LIMIT_MB = 56


def _cp(sem, vmem_mb=VMEM_LIMIT_MB):
    return pltpu.CompilerParams(dimension_semantics=sem, vmem_limit_bytes=vmem_mb * 2**20)


def _tile(n, pref, align=8):
    if n <= pref:
        return n
    t = (pref // align) * align
    while t >= align:
        if n % t == 0:
            return t
        t -= align
    raise ValueError(f"no tile for {n} (pref {pref}, align {align})")


def _round_up(n, m):
    return (n + m - 1) // m * m


def _rms(x, g):
    ms = jnp.mean(x * x, axis=-1, keepdims=True)
    return x * lax.rsqrt(ms + RMS_EPS) * g


def _rmsnorm_body(x_ref, g_ref, o_ref):
    o_ref[...] = _rms(x_ref[...], g_ref[...]).astype(o_ref.dtype)


def _rmsnorm(x, g, out_dtype):
    m, d = x.shape
    tm = _tile(m, 256)
    return pl.pallas_call(
        _rmsnorm_body,
        grid=(m // tm,),
        in_specs=[pl.BlockSpec((tm, d), lambda i: (i, 0)),
                  pl.BlockSpec((1, d), lambda i: (0, 0))],
        out_specs=pl.BlockSpec((tm, d), lambda i: (i, 0)),
        out_shape=jax.ShapeDtypeStruct((m, d), out_dtype),
        compiler_params=_cp(("parallel",)),
        name="rmsnorm",
    )(x, g.reshape(1, d))


def _mm_body(*refs, nk, has_res):
    if has_res:
        a_ref, w_ref, r_ref, o_ref, acc_ref = refs
    else:
        a_ref, w_ref, o_ref, acc_ref = refs
        r_ref = None
    k = pl.program_id(2)
    part = jnp.dot(a_ref[...], w_ref[...], preferred_element_type=F32)

    @pl.when(k == 0)
    def _():
        acc_ref[...] = part

    @pl.when(k > 0)
    def _():
        acc_ref[...] += part

    @pl.when(k == nk - 1)
    def _():
        r = acc_ref[...]
        if has_res:
            r = r + r_ref[...]
        o_ref[...] = r.astype(o_ref.dtype)


def _mm(a, w, lead, n, out_dtype, res=None, tm=1024, tn=1024, tk=1024, name="mm"):
    m, kdim = a.shape
    tm, tn, tk = _tile(m, tm), _tile(n, tn, LANE), _tile(kdim, tk, LANE)
    nk = kdim // tk
    nlead = len(lead)
    in_specs = [pl.BlockSpec((tm, tk), lambda i, j, k: (i, k)),
                pl.BlockSpec((None,) * nlead + (tk, tn), lambda i, j, k: lead + (k, j))]
    args = [a, w]
    if res is not None:
        in_specs.append(pl.BlockSpec((tm, tn), lambda i, j, k: (i, j)))
        args.append(res)
    return pl.pallas_call(
        functools.partial(_mm_body, nk=nk, has_res=res is not None),
        grid=(m // tm, n // tn, nk),
        in_specs=in_specs,
        out_specs=pl.BlockSpec((tm, tn), lambda i, j, k: (i, j)),
        out_shape=jax.ShapeDtypeStruct((m, n), out_dtype),
        scratch_shapes=[pltpu.VMEM((tm, tn), F32)],
        compiler_params=_cp(("parallel", "parallel", "arbitrary")),
        name=name,
    )(*args)


def _glu_body(*refs, nk, has_gate, blocks_per_expert):
    if has_gate:
        a_ref, wg_ref, wu_ref, gate_ref, o_ref, accg_ref, accu_ref = refs
    else:
        a_ref, wg_ref, wu_ref, o_ref, accg_ref, accu_ref = refs
        gate_ref = None
    k = pl.program_id(2)
    a = a_ref[...]
    pg = jnp.dot(a, wg_ref[...], preferred_element_type=F32)
    pu = jnp.dot(a, wu_ref[...], preferred_element_type=F32)

    @pl.when(k == 0)
    def _():
        accg_ref[...] = pg
        accu_ref[...] = pu

    @pl.when(k > 0)
    def _():
        accg_ref[...] += pg
        accu_ref[...] += pu

    @pl.when(k == nk - 1)
    def _():
        h = jax.nn.silu(accg_ref[...]) * accu_ref[...]
        if has_gate:
            e = (pl.program_id(1) // blocks_per_expert).astype(F32)
            gates = gate_ref[...]
            lane = lax.broadcasted_iota(jnp.int32, gates.shape, 1).astype(F32)
            h = h * jnp.sum(jnp.where(lane == e, gates, 0.0), axis=-1, keepdims=True)
        o_ref[...] = h.astype(o_ref.dtype)


def _glu_up(a, wg, wu, lead, n_experts, f, gates=None, tm=1024, tn=1024, tk=1024):
    m, kdim = a.shape
    tm, tn, tk = _tile(m, tm), _tile(f, tn, LANE), _tile(kdim, tk, LANE)
    nk = kdim // tk
    bpe = f // tn
    nlead = len(lead)
    if gates is not None:
        w_spec = pl.BlockSpec((None,) * (nlead + 1) + (tk, tn),
                              lambda i, j, k: lead + (j // bpe, k, j % bpe))
    else:
        w_spec = pl.BlockSpec((None,) * nlead + (tk, tn), lambda i, j, k: lead + (k, j))
    in_specs = [pl.BlockSpec((tm, tk), lambda i, j, k: (i, k)), w_spec, w_spec]
    args = [a, wg, wu]
    if gates is not None:
        in_specs.append(pl.BlockSpec((tm, LANE), lambda i, j, k: (i, 0)))
        args.append(gates)
    return pl.pallas_call(
        functools.partial(_glu_body, nk=nk, has_gate=gates is not None, blocks_per_expert=bpe),
        grid=(m // tm, n_experts * bpe, nk),
        in_specs=in_specs,
        out_specs=pl.BlockSpec((tm, tn), lambda i, j, k: (i, j)),
        out_shape=jax.ShapeDtypeStruct((m, n_experts * f), BF16),
        scratch_shapes=[pltpu.VMEM((tm, tn), F32), pltpu.VMEM((tm, tn), F32)],
        compiler_params=_cp(("parallel", "parallel", "arbitrary")),
        name="glu_up",
    )(*args)


def _rope_tables(seq_len):
    half = ROPE_DIM // 2
    inv_freq = 1.0 / (ROPE_THETA ** (jnp.arange(half, dtype=F32) * 2.0 / ROPE_DIM))
    ang = jnp.arange(seq_len, dtype=F32)[:, None] * inv_freq[None, :]
    cos, sin = jnp.cos(ang), jnp.sin(ang)
    one = jnp.ones((seq_len, DA_HEAD_DIM - ROPE_DIM), F32)
    zero = jnp.zeros((seq_len, DA_HEAD_DIM - ROPE_DIM), F32)
    zh = jnp.zeros_like(sin)
    comp = lambda a, b, rest: jnp.concatenate([a, b, rest], axis=-1)
    c = comp(cos, cos, one)
    s_from_lower = comp(zh, sin, zero)
    s_from_upper = comp(-sin, zh, zero)
    two = lambda t: jnp.concatenate([t, t], axis=-1)
    return two(c), two(s_from_lower), two(s_from_upper)


def _rope_body(x_ref, c_ref, sl_ref, su_ref, o_ref, *, q_scale):
    c, sl, su = c_ref[...], sl_ref[...], su_ref[...]
    n_blocks = 2 * DA_HEADS
    for j in range(n_blocks):
        t = x_ref[:, j * LANE:(j + 1) * LANE].astype(F32)
        r = t * c + pltpu.roll(t, ROPE_DIM // 2, 1) * sl + pltpu.roll(t, LANE - ROPE_DIM // 2, 1) * su
        if j < DA_HEADS:
            r = r * q_scale
        o_ref[:, j * LANE:(j + 1) * LANE] = r.astype(o_ref.dtype)


def _rope(u, row0, nb, seq_len, tables):
    tl = _tile(seq_len, 512, 16)
    nl = seq_len // tl
    r0 = row0 // tl
    width = 2 * MIX
    tab_spec = pl.BlockSpec((tl, LANE), lambda i: (i % nl, 0))
    return pl.pallas_call(
        functools.partial(_rope_body, q_scale=DA_HEAD_DIM ** -0.5),
        grid=(nb * nl,),
        in_specs=[pl.BlockSpec((tl, width), lambda i: (r0 + i, 0)), tab_spec, tab_spec, tab_spec],
        out_specs=pl.BlockSpec((tl, width), lambda i: (i, 0)),
        out_shape=jax.ShapeDtypeStruct((nb * seq_len, width), BF16),
        compiler_params=_cp(("parallel",)),
        name="rope",
    )(u, *tables)


def _attn_body(lam_ref, g_ref, q_ref, k_ref, v_ref, o_ref, m_ref, l_ref, acc_ref, *, tk, nkv, post_scale):
    q = q_ref[...]
    lane = lax.broadcasted_iota(jnp.int32, q.shape, 1)
    zero = jnp.zeros_like(q)
    qs = (jnp.where(lane < DA_HEAD_DIM, q, zero), jnp.where(lane >= DA_HEAD_DIM, q, zero))
    m_ref[...] = jnp.full(m_ref.shape, -jnp.inf, F32)
    l_ref[...] = jnp.zeros(l_ref.shape, F32)
    acc_ref[...] = jnp.zeros(acc_ref.shape, F32)

    def step(j, carry):
        start = pl.multiple_of(j * tk, tk)
        kk = k_ref[pl.ds(start, tk), :]
        vv = v_ref[pl.ds(start, tk), :]
        for c in range(2):
            s = lax.dot_general(qs[c], kk, (((1,), (1,)), ((), ())), preferred_element_type=F32)
            m_prev = m_ref[c]
            m_new = jnp.maximum(m_prev, jnp.max(s, axis=-1, keepdims=True))
            alpha = jnp.exp(m_prev - m_new)
            p = jnp.exp(s - m_new)
            l_ref[c] = alpha * l_ref[c] + jnp.sum(p, axis=-1, keepdims=True)
            acc_ref[c] = alpha * acc_ref[c] + jnp.dot(p.astype(BF16), vv, preferred_element_type=F32)
            m_ref[c] = m_new
        return carry

    lax.fori_loop(0, nkv, step, 0)
    o = acc_ref[0] / l_ref[0] - lam_ref[...] * (acc_ref[1] / l_ref[1])
    o_ref[...] = (_rms(o, g_ref[...]) * post_scale).astype(o_ref.dtype)


def _attention(qk, u, row0, nb, seq_len, lam, g_subln, lam_init):
    tq = _tile(seq_len, 512, 16)
    tk = _tile(seq_len, 512, 16)
    nq = seq_len // tq
    assert row0 % seq_len == 0
    rb = row0 // seq_len
    return pl.pallas_call(
        functools.partial(_attn_body, tk=tk, nkv=seq_len // tk, post_scale=1.0 - lam_init),
        grid=(nb, DA_HEADS, nq),
        in_specs=[pl.BlockSpec((1, LANE), lambda b, h, i: (0, 0)),
                  pl.BlockSpec((1, LANE), lambda b, h, i: (0, 0)),
                  pl.BlockSpec((tq, LANE), lambda b, h, i: (b * nq + i, h)),
                  pl.BlockSpec((seq_len, LANE), lambda b, h, i: (b, DA_HEADS + h)),
                  pl.BlockSpec((seq_len, LANE), lambda b, h, i: (rb + b, COL_V // LANE + h))],
        out_specs=pl.BlockSpec((tq, LANE), lambda b, h, i: (b * nq + i, h)),
        out_shape=jax.ShapeDtypeStruct((nb * seq_len, MIX), BF16),
        scratch_shapes=[pltpu.VMEM((2, tq, 1), F32), pltpu.VMEM((2, tq, 1), F32),
                        pltpu.VMEM((2, tq, DA_V_DIM), F32)],
        compiler_params=_cp(("parallel", "parallel", "arbitrary")),
        name="diff_attention",
    )(jnp.full((1, LANE), lam, F32), g_subln.reshape(1, DA_V_DIM).astype(F32), qk, qk, u)


def _dft_cs(n):
    idx = jnp.arange(n, dtype=jnp.int32)
    ang = ((idx[:, None] * idx[None, :]) % n).astype(F32) * (2.0 * math.pi / n)
    return jnp.cos(ang), jnp.sin(ang)


def _fourier_consts(seq_len):
    n1 = 1 << ((seq_len.bit_length() - 1 + 1) // 2)
    n2 = seq_len // n1
    assert n1 * n2 == seq_len and n1 % 16 == 0 and n2 % 16 == 0
    cc, sc = _dft_cs(FN_GROUP_DIM)
    w_chan = jnp.concatenate([cc, -sc], axis=1).astype(BF16)
    c1, s1 = _dft_cs(n1)
    m1 = jnp.concatenate([jnp.concatenate([c1, s1], 1), jnp.concatenate([-s1, c1], 1)], 0).astype(BF16)
    k1 = jnp.arange(n1, dtype=jnp.int32)[:, None]
    i2 = jnp.arange(n2, dtype=jnp.int32)[None, :]
    ang = ((k1 * i2) % seq_len).astype(F32) * (2.0 * math.pi / seq_len)
    expand = lambda t: jnp.repeat(t, FN_GROUP_DIM, axis=1)
    c2, s2 = _dft_cs(n2)
    return dict(n1=n1, n2=n2, w_chan=w_chan, m1=m1, tw_c=expand(jnp.cos(ang)), tw_s=expand(jnp.sin(ang)),
                c2=c2.astype(BF16), s2=s2.astype(BF16))


def _fn_chan_body(x_ref, w_ref, o_ref):
    y = jnp.dot(x_ref[...], w_ref[...], preferred_element_type=F32)
    o_ref[0] = y[:, :FN_GROUP_DIM].astype(o_ref.dtype)
    o_ref[1] = y[:, FN_GROUP_DIM:].astype(o_ref.dtype)


def _fn_stage1_body(y_ref, m1_ref, tc_ref, ts_ref, o_ref, *, n1):
    a = jnp.dot(m1_ref[...], y_ref[...], preferred_element_type=F32)
    ar, ai = a[:n1], a[n1:]
    tc, ts = tc_ref[...], ts_ref[...]
    o_ref[:n1, :] = (ar * tc + ai * ts).astype(o_ref.dtype)
    o_ref[n1:, :] = (ai * tc - ar * ts).astype(o_ref.dtype)


def _fn_stage2_body(a_ref, c2_ref, s2_ref, o_ref, *, kb, norm):
    c2, s2 = c2_ref[...], s2_ref[...]
    for j in range(kb):
        for g in range(FN_GROUPS):
            r = (jnp.dot(c2, a_ref[g, 0, j], preferred_element_type=F32)
                 + jnp.dot(s2, a_ref[g, 1, j], preferred_element_type=F32))
            col = (j * FN_GROUPS + g) * FN_GROUP_DIM
            o_ref[:, col:col + FN_GROUP_DIM] = (r * norm).astype(o_ref.dtype)


def _fourier_mix(u, row0, nb, seq_len, fc):
    n1, n2 = fc["n1"], fc["n2"]
    gd = FN_GROUP_DIM
    tl = _tile(seq_len, 512, 16)
    nl = seq_len // tl
    r0 = row0 // tl
    y = pl.pallas_call(
        _fn_chan_body,
        grid=(nb, FN_GROUPS, nl),
        in_specs=[pl.BlockSpec((tl, gd), lambda b, g, i: (r0 + b * nl + i, COL_FN // gd + g)),
                  pl.BlockSpec((gd, 2 * gd), lambda b, g, i: (0, 0))],
        out_specs=pl.BlockSpec((None, None, 2, tl, gd), lambda b, g, i: (b, g, 0, i, 0)),
        out_shape=jax.ShapeDtypeStruct((nb, FN_GROUPS, 2, seq_len, gd), BF16),
        compiler_params=_cp(("parallel", "parallel", "parallel")),
        name="fnet_channel_dft",
    )(u, fc["w_chan"])

    cols = n2 * gd
    tc = _tile(cols, 2048, LANE)
    a = pl.pallas_call(
        functools.partial(_fn_stage1_body, n1=n1),
        grid=(nb, FN_GROUPS, cols // tc),
        in_specs=[pl.BlockSpec((None, None, 2 * n1, tc), lambda b, g, i: (b, g, 0, i)),
                  pl.BlockSpec((2 * n1, 2 * n1), lambda b, g, i: (0, 0)),
                  pl.BlockSpec((n1, tc), lambda b, g, i: (0, i)),
                  pl.BlockSpec((n1, tc), lambda b, g, i: (0, i))],
        out_specs=pl.BlockSpec((None, None, 2 * n1, tc), lambda b, g, i: (b, g, 0, i)),
        out_shape=jax.ShapeDtypeStruct((nb, FN_GROUPS, 2 * n1, cols), BF16),
        compiler_params=_cp(("parallel", "parallel", "parallel")),
        name="fnet_seq_dft_stage1",
    )(y.reshape(nb, FN_GROUPS, 2 * n1, cols), fc["m1"], fc["tw_c"], fc["tw_s"])

    kb = _tile(n1, 8, 1)
    out = pl.pallas_call(
        functools.partial(_fn_stage2_body, kb=kb, norm=1.0 / math.sqrt(seq_len * gd)),
        grid=(nb, n1 // kb),
        in_specs=[pl.BlockSpec((None, FN_GROUPS, 2, kb, n2, gd), lambda b, i: (b, 0, 0, i, 0, 0)),
                  pl.BlockSpec((n2, n2), lambda b, i: (0, 0)),
                  pl.BlockSpec((n2, n2), lambda b, i: (0, 0))],
        out_specs=pl.BlockSpec((None, n2, kb * MIX), lambda b, i: (b, 0, i)),
        out_shape=jax.ShapeDtypeStruct((nb, n2, n1 * MIX), BF16),
        compiler_params=_cp(("parallel", "parallel")),
        name="fnet_seq_dft_stage2",
    )(a.reshape(nb, FN_GROUPS, 2, n1, n2, gd), fc["c2"], fc["s2"])
    return out.reshape(nb * seq_len, MIX)


def _conv_body(ap_ref, ac_ref, an_ref, gp_ref, gc_ref, gn_ref, w_ref, b_ref, lg_ref, lb_ref, o_ref, z_ref,
               *, tl, nl):
    i = pl.program_id(1)
    glu = lambda a_ref, g_ref: a_ref[...].astype(F32) * jax.nn.sigmoid(g_ref[...].astype(F32))
    z_ref[0:CONV_HALO, :] = jnp.where(i > 0, glu(ap_ref, gp_ref), 0.0)
    z_ref[CONV_HALO:CONV_HALO + tl, :] = glu(ac_ref, gc_ref)
    z_ref[CONV_HALO + tl:2 * CONV_HALO + tl, :] = jnp.where(i < nl - 1, glu(an_ref, gn_ref), 0.0)
    acc = jnp.broadcast_to(b_ref[...], (tl, MIX))
    first = CONV_HALO - CONV_TAPS // 2
    for j in range(CONV_TAPS):
        acc = acc + w_ref[j:j + 1, :] * z_ref[first + j:first + j + tl, :]
    mu = jnp.mean(acc, axis=-1, keepdims=True)
    xc = acc - mu
    var = jnp.mean(xc * xc, axis=-1, keepdims=True)
    y = xc * lax.rsqrt(var + LN_EPS) * lg_ref[...] + lb_ref[...]
    o_ref[...] = jax.nn.silu(y).astype(o_ref.dtype)


def _conformer_conv(u, row0, nb, seq_len, w_dw, b_dw, ln_g, ln_b):
    m = u.shape[0]
    tl = _tile(seq_len, 256, CONV_HALO)
    nl = seq_len // tl
    r0 = row0 // tl
    hb = tl // CONV_HALO
    last_hb = m // CONV_HALO - 1
    ca, cg = COL_CV // MIX, COL_CV // MIX + 1

    def cur(c):
        return pl.BlockSpec((tl, MIX), lambda b, i: (r0 + b * nl + i, c))

    def prev(c):
        return pl.BlockSpec((CONV_HALO, MIX), lambda b, i: (jnp.maximum((r0 + b * nl + i) * hb - 1, 0), c))

    def nxt(c):
        return pl.BlockSpec((CONV_HALO, MIX), lambda b, i: (jnp.minimum((r0 + b * nl + i + 1) * hb, last_hb), c))

    vec = pl.BlockSpec((1, MIX), lambda b, i: (0, 0))
    return pl.pallas_call(
        functools.partial(_conv_body, tl=tl, nl=nl),
        grid=(nb, nl),
        in_specs=[prev(ca), cur(ca), nxt(ca), prev(cg), cur(cg), nxt(cg),
                  pl.BlockSpec((CONV_TAPS, MIX), lambda b, i: (0, 0)), vec, vec, vec],
        out_specs=pl.BlockSpec((tl, MIX), lambda b, i: (b * nl + i, 0)),
        out_shape=jax.ShapeDtypeStruct((nb * seq_len, MIX), BF16),
        scratch_shapes=[pltpu.VMEM((tl + 2 * CONV_HALO, MIX), F32)],
        compiler_params=_cp(("parallel", "arbitrary")),
        name="conformer_conv",
    )(u, u, u, u, u, u, w_dw, b_dw.reshape(1, MIX), ln_g.reshape(1, MIX), ln_b.reshape(1, MIX))


def _log_sigmoid(z):
    return jnp.minimum(z, 0.0) - jnp.log1p(jnp.exp(-jnp.abs(z)))


def _gla_scan_body(qf_ref, kf_ref, vf_ref, gaf_ref, qb_ref, kb_ref, vb_ref, gab_ref, w2_ref, b2_ref,
                   of_ref, ob_ref, st_ref):
    @pl.when(pl.program_id(1) == 0)
    def _():
        st_ref[...] = jnp.zeros(st_ref.shape, F32)

    c = GLA_CHUNK
    row = lax.broadcasted_iota(jnp.int32, (c, c), 0)
    col = lax.broadcasted_iota(jnp.int32, (c, c), 1)
    kw = GLA_HEADS * GLA_DK
    dirs = ((qf_ref, kf_ref, vf_ref, gaf_ref, of_ref, col <= row),
            (qb_ref, kb_ref, vb_ref, gab_ref, ob_ref, col >= row))
    for d, (q_ref, k_ref, v_ref, ga_ref, o_ref, keep) in enumerate(dirs):
        z = jnp.dot(ga_ref[...], w2_ref[:, d * kw:(d + 1) * kw], preferred_element_type=F32)
        g_all = _log_sigmoid(z + b2_ref[:, d * kw:(d + 1) * kw]) / GLA_TAU
        tri = keep.astype(F32)
        for h in range(GLA_HEADS):
            ks = slice(h * GLA_DK, (h + 1) * GLA_DK)
            vs = slice(h * GLA_DV, (h + 1) * GLA_DV)
            g = g_all[:, ks]
            b = jnp.dot(tri, g, preferred_element_type=F32, precision=lax.Precision.HIGHEST)
            b_tot = jnp.sum(g, axis=0, keepdims=True)
            q = q_ref[:, ks].astype(F32) * (GLA_DK ** -0.5)
            k = k_ref[:, ks].astype(F32)
            v = v_ref[:, vs]
            q_dec = (q * jnp.exp(b)).astype(BF16)
            k_inv = (k * jnp.exp(-b)).astype(BF16)
            k_end = (k * jnp.exp(b_tot - b)).astype(BF16)
            st = st_ref[d, h]
            o_inter = lax.dot_general(q_dec, st.astype(BF16), (((1,), (1,)), ((), ())),
                                      preferred_element_type=F32)
            a = lax.dot_general(q_dec, k_inv, (((1,), (1,)), ((), ())), preferred_element_type=F32)
            a = jnp.where(keep, a, 0.0)
            o_intra = jnp.dot(a.astype(BF16), v, preferred_element_type=F32)
            o_ref[:, vs] = o_inter + o_intra
            st_ref[d, h] = st * jnp.exp(b_tot) + lax.dot_general(
                v, k_end, (((0,), (0,)), ((), ())), preferred_element_type=F32)


def _gla_post_body(of_ref, ob_ref, r_ref, g_ref, o_ref):
    g = g_ref[...]
    for h in range(GLA_HEADS):
        vs = slice(h * GLA_DV, (h + 1) * GLA_DV)
        o = _rms(of_ref[:, vs] + ob_ref[:, vs], g)
        o_ref[:, vs] = (o * jax.nn.silu(r_ref[:, vs].astype(F32))).astype(o_ref.dtype)


def _gla(u, ga, row0, nb, seq_len, w2, b2, g_norm):
    c = GLA_CHUNK
    n = seq_len // c
    r0 = row0 // c
    kw = GLA_HEADS * GLA_DK
    fwd = lambda col: (lambda b, i: (r0 + b * n + i, col))
    bwd = lambda col: (lambda b, i: (r0 + b * n + n - 1 - i, col))
    specs = []
    for imap in (fwd, bwd):
        specs += [pl.BlockSpec((c, kw), imap(COL_GQ // kw)), pl.BlockSpec((c, kw), imap(COL_GK // kw)),
                  pl.BlockSpec((c, MIX), imap(COL_GV // MIX)), pl.BlockSpec((c, LANE), imap(0))]
    specs += [pl.BlockSpec((LANE, 2 * kw), lambda b, i: (0, 0)), pl.BlockSpec((1, 2 * kw), lambda b, i: (0, 0))]
    o_shape = jax.ShapeDtypeStruct((nb * seq_len, MIX), F32)
    o_f, o_b = pl.pallas_call(
        _gla_scan_body,
        grid=(nb, n),
        in_specs=specs,
        out_specs=[pl.BlockSpec((c, MIX), lambda b, i: (b * n + i, 0)),
                   pl.BlockSpec((c, MIX), lambda b, i: (b * n + n - 1 - i, 0))],
        out_shape=[o_shape, o_shape],
        scratch_shapes=[pltpu.VMEM((2, GLA_HEADS, GLA_DV, GLA_DK), F32)],
        compiler_params=_cp(("parallel", "arbitrary")),
        name="gla_scan",
    )(u, u, u, ga, u, u, u, ga, w2, b2)

    tl = _tile(seq_len, 256, 16)
    nl = seq_len // tl
    rr = row0 // tl
    row_spec = pl.BlockSpec((tl, MIX), lambda i: (i, 0))
    return pl.pallas_call(
        _gla_post_body,
        grid=(nb * nl,),
        in_specs=[row_spec, row_spec, pl.BlockSpec((tl, MIX), lambda i: (rr + i, COL_GR // MIX)),
                  pl.BlockSpec((1, GLA_DV), lambda i: (0, 0))],
        out_specs=row_spec,
        out_shape=jax.ShapeDtypeStruct((nb * seq_len, MIX), BF16),
        compiler_params=_cp(("parallel",)),
        name="gla_post",
    )(o_f, o_b, u, g_norm.reshape(1, GLA_DV))


def _merge_body(h_ref, wg_ref, bg_ref, z_ref, wb_ref, o_ref, accg_ref, accm_ref, *, nk, n_branch):
    i = pl.program_id(2)
    k = pl.program_id(3)
    part = jnp.dot(h_ref[...], wg_ref[...], preferred_element_type=F32)

    @pl.when(k == 0)
    def _():
        accg_ref[...] = part

    @pl.when(k > 0)
    def _():
        accg_ref[...] += part

    @pl.when(k == nk - 1)
    def _():
        gate = jax.nn.sigmoid(accg_ref[...] + bg_ref[...])
        contrib = gate * jnp.dot(z_ref[...], wb_ref[...], preferred_element_type=F32)

        @pl.when(i == 0)
        def _():
            accm_ref[...] = contrib

        @pl.when(i > 0)
        def _():
            accm_ref[...] += contrib

        @pl.when(i == n_branch - 1)
        def _():
            o_ref[...] = accm_ref[...].astype(o_ref.dtype)


def _merge(h, z, w_gate, b_gate, w_branch, layer, tm=1024, tn=1024, tk=1024):
    m, d = h.shape
    n_branch = z.shape[0]
    tm, tn, tk = _tile(m, tm), _tile(d, tn, LANE), _tile(d, tk, LANE)
    nk = d // tk
    return pl.pallas_call(
        functools.partial(_merge_body, nk=nk, n_branch=n_branch),
        grid=(m // tm, d // tn, n_branch, nk),
        in_specs=[pl.BlockSpec((tm, tk), lambda a, j, i, k: (a, k)),
                  pl.BlockSpec((None, None, tk, tn), lambda a, j, i, k: (layer, i, k, j)),
                  pl.BlockSpec((None, None, 1, tn), lambda a, j, i, k: (layer, i, 0, j)),
                  pl.BlockSpec((None, tm, MIX), lambda a, j, i, k: (i, a, 0)),
                  pl.BlockSpec((None, None, MIX, tn), lambda a, j, i, k: (layer, i, 0, j))],
        out_specs=pl.BlockSpec((tm, tn), lambda a, j, i, k: (a, j)),
        out_shape=jax.ShapeDtypeStruct((m, d), BF16),
        scratch_shapes=[pltpu.VMEM((tm, tn), F32), pltpu.VMEM((tm, tn), F32)],
        compiler_params=_cp(("parallel", "parallel", "arbitrary", "arbitrary")),
        name="gated_merge",
    )(h, w_gate, b_gate.reshape(b_gate.shape[0], n_branch, 1, d), z, w_branch)


def _router_body(x_ref, g_ref, wr_ref, h_ref, gate_ref, *, n_experts):
    hb = _rms(x_ref[...], g_ref[...]).astype(BF16)
    h_ref[...] = hb
    logits = jnp.dot(hb, wr_ref[...], preferred_element_type=F32)
    lane = lax.broadcasted_iota(jnp.int32, logits.shape, 1).astype(F32)
    neg = jnp.float32(-jnp.inf)
    big = jnp.float32(LANE)
    lg = jnp.where(lane < n_experts, logits, neg)
    m1 = jnp.max(lg, axis=-1, keepdims=True)
    i1 = jnp.min(jnp.where(lg == m1, lane, big), axis=-1, keepdims=True)
    lg2 = jnp.where(lane == i1, neg, lg)
    m2 = jnp.max(lg2, axis=-1, keepdims=True)
    i2 = jnp.min(jnp.where(lg2 == m2, lane, big), axis=-1, keepdims=True)
    e2 = jnp.exp(m2 - m1)
    denom = 1.0 + e2
    gate_ref[...] = jnp.where(lane == i1, 1.0 / denom, 0.0) + jnp.where(lane == i2, e2 / denom, 0.0)


def _norm_router(x, g, w_router_pad, n_experts):
    m, d = x.shape
    tm = _tile(m, 256)
    return pl.pallas_call(
        functools.partial(_router_body, n_experts=n_experts),
        grid=(m // tm,),
        in_specs=[pl.BlockSpec((tm, d), lambda i: (i, 0)),
                  pl.BlockSpec((1, d), lambda i: (0, 0)),
                  pl.BlockSpec((d, LANE), lambda i: (0, 0))],
        out_specs=[pl.BlockSpec((tm, d), lambda i: (i, 0)), pl.BlockSpec((tm, LANE), lambda i: (i, 0))],
        out_shape=[jax.ShapeDtypeStruct((m, d), BF16), jax.ShapeDtypeStruct((m, LANE), F32)],
        compiler_params=_cp(("parallel",)),
        name="norm_router",
    )(x, g.reshape(1, d), w_router_pad)


def kernel(x_prompt, x_sample, norm_mix, w_in, da_lambda, da_subln_g, cv_dw_w, cv_dw_b, cv_ln_g, cv_ln_b,
           gla_w_a2, gla_b_a2, gla_norm_g, w_gate, b_gate, w_branch, w_out, norm_ffn, ffn_w_gate, ffn_w_up,
           ffn_w_down, moe_router, moe_w_gate, moe_w_up, moe_w_down, norm_final):
    depth = norm_mix.shape[0]
    d = x_prompt.shape[-1]
    seqs = []
    row = 0
    for xs in (x_prompt, x_sample):
        seqs.append((row, xs.shape[0], xs.shape[1]))
        row += xs.shape[0] * xs.shape[1]
    x = jnp.concatenate([x_prompt.reshape(-1, d), x_sample.reshape(-1, d)], axis=0)

    w_in_b = w_in.astype(BF16)
    w_ga_b = jnp.pad(w_in[:, :, COL_GA:], ((0, 0), (0, 0), (0, LANE - 2 * GLA_RANK))).astype(BF16)
    w_gate_b = w_gate.astype(BF16)
    w_branch_b = w_branch.astype(BF16)
    w_out_b = w_out.astype(BF16)
    d_ff = ffn_w_gate.shape[-1]
    f_pad = _round_up(d_ff, 1024 if d_ff >= 1024 else LANE) - d_ff
    ffn_wg_b = jnp.pad(ffn_w_gate, ((0, 0), (0, 0), (0, f_pad))).astype(BF16)
    ffn_wu_b = jnp.pad(ffn_w_up, ((0, 0), (0, 0), (0, f_pad))).astype(BF16)
    ffn_wd_b = jnp.pad(ffn_w_down, ((0, 0), (0, f_pad), (0, 0))).astype(BF16)
    n_experts, f_exp = moe_w_gate.shape[1], moe_w_gate.shape[-1]
    moe_wg_b = moe_w_gate.astype(BF16)
    moe_wu_b = moe_w_up.astype(BF16)
    moe_wd_b = moe_w_down.astype(BF16).reshape(moe_w_down.shape[0], n_experts * f_exp, d)
    router_b = jnp.pad(moe_router, ((0, 0), (0, 0), (0, LANE - n_experts))).astype(BF16)
    kw = GLA_HEADS * GLA_DK
    w2 = jnp.zeros((depth, LANE, 2 * kw), F32)
    w2 = w2.at[:, :GLA_RANK, :kw].set(gla_w_a2[:, 0]).at[:, GLA_RANK:2 * GLA_RANK, kw:].set(gla_w_a2[:, 1])
    w2 = w2.astype(BF16)
    b2 = gla_b_a2.reshape(depth, 1, 2 * kw)

    rope_tabs = {s[2]: _rope_tables(s[2]) for s in seqs}
    fn_consts = {s[2]: _fourier_consts(s[2]) for s in seqs}

    for l in range(depth):
        h = _rmsnorm(x, norm_mix[l], BF16)
        u = _mm(h, w_in_b, (l,), COL_GA, BF16, name="in_proj")
        ga = _mm(h, w_ga_b, (l,), LANE, BF16, name="gla_gate_proj")
        lam_init = 0.8 - 0.6 * math.exp(-0.3 * l)
        lp = da_lambda[l].astype(F32)
        lam = jnp.exp(jnp.sum(lp[0] * lp[1])) - jnp.exp(jnp.sum(lp[2] * lp[3])) + lam_init
        zs = [[], [], [], []]
        for row0, nb, seq_len in seqs:
            qk = _rope(u, row0, nb, seq_len, rope_tabs[seq_len])
            zs[0].append(_attention(qk, u, row0, nb, seq_len, lam, da_subln_g[l], lam_init))
            zs[1].append(_fourier_mix(u, row0, nb, seq_len, fn_consts[seq_len]))
            zs[2].append(_conformer_conv(u, row0, nb, seq_len, cv_dw_w[l], cv_dw_b[l], cv_ln_g[l], cv_ln_b[l]))
            zs[3].append(_gla(u, ga, row0, nb, seq_len, w2[l], b2[l], gla_norm_g[l]))
        z = jnp.stack([jnp.concatenate(parts, axis=0) for parts in zs], axis=0)
        merged = _merge(h, z, w_gate_b, b_gate, w_branch_b, l)
        x = _mm(merged, w_out_b, (l,), d, F32, res=x, name="out_proj")
        j = l // 2
        if l % 2 == 0:
            h2 = _rmsnorm(x, norm_ffn[l], BF16)
            mid = _glu_up(h2, ffn_wg_b, ffn_wu_b, (j,), 1, d_ff + f_pad)
            x = _mm(mid, ffn_wd_b, (j,), d, F32, res=x, name="ffn_down")
        else:
            h2, gates = _norm_router(x, norm_ffn[l], router_b[j], n_experts)
            mid = _glu_up(h2, moe_wg_b, moe_wu_b, (j,), n_experts, f_exp, gates=gates, tn=896)
            x = _mm(mid, moe_wd_b, (j,), d, F32, res=x, name="moe_down")
    y = _rmsnorm(x, norm_final, F32)
    n_prompt = seqs[0][1] * seqs[0][2]
    return y[:n_prompt].reshape(x_prompt.shape), y[n_prompt:].reshape(x_sample.shape)
```

```python
import functools
import math

import jax
import jax.numpy as jnp
from jax import lax
from jax.experimental import pallas as pl
from jax.experimental.pallas import tpu as pltpu

F32 = jnp.float32
BF16 = jnp.bfloat16

MIX = 1024
DA_HEADS = 8
DA_HEAD_DIM = 64
DA_V_DIM = 128
ROPE_DIM = 16
ROPE_THETA = 500000.0
FN_GROUPS = 4
FN_GROUP_DIM = 256
CONV_TAPS = 31
CONV_HALO = 16
GLA_HEADS = 4
GLA_DK = 128
GLA_DV = 256
GLA_RANK = 16
GLA_TAU = 16.0
GLA_CHUNK = 64
TOP_K = 2
RMS_EPS = 1e-6
LN_EPS = 1e-5

COL_Q = 0
COL_K = 1024
COL_V = 2048
COL_FN = 3072
COL_CV = 4096
COL_GQ = 6144
COL_GK = 6656
COL_GV = 7168
COL_GR = 8192
COL_GA = 9216
LANE = 128

VMEM_LIMIT_MB = 56


def _cp(sem, vmem_mb=VMEM_LIMIT_MB):
    return pltpu.CompilerParams(dimension_semantics=sem, vmem_limit_bytes=vmem_mb * 2**20)


def _tile(n, pref, align=8):
    if n <= pref:
        return n
    t = (pref // align) * align
    while t >= align:
        if n % t == 0:
            return t
        t -= align
    raise ValueError(f"no tile for {n} (pref {pref}, align {align})")


def _round_up(n, m):
    return (n + m - 1) // m * m


def _rms(x, g):
    ms = jnp.mean(x * x, axis=-1, keepdims=True)
    return x * lax.rsqrt(ms + RMS_EPS) * g


def _rmsnorm_body(x_ref, g_ref, o_ref):
    o_ref[...] = _rms(x_ref[...], g_ref[...]).astype(o_ref.dtype)


def _rmsnorm(x, g, out_dtype):
    m, d = x.shape
    tm = _tile(m, 256)
    return pl.pallas_call(
        _rmsnorm_body,
        grid=(m // tm,),
        in_specs=[pl.BlockSpec((tm, d), lambda i: (i, 0)),
                  pl.BlockSpec((1, d), lambda i: (0, 0))],
        out_specs=pl.BlockSpec((tm, d), lambda i: (i, 0)),
        out_shape=jax.ShapeDtypeStruct((m, d), out_dtype),
        compiler_params=_cp(("parallel",)),
        name="rmsnorm",
    )(x, g.reshape(1, d))


def _mm_body(*refs, nk, has_res):
    if has_res:
        a_ref, w_ref, r_ref, o_ref = refs
    else:
        a_ref, w_ref, o_ref = refs
        r_ref = None

    def first():
        r = jnp.dot(a_ref[...], w_ref[...], preferred_element_type=F32)
        if has_res:
            r = r + r_ref[...]
        o_ref[...] = r.astype(o_ref.dtype)

    if nk == 1:
        first()
        return
    k = pl.program_id(2)
    pl.when(k == 0)(first)

    @pl.when(k > 0)
    def _():
        o_ref[...] += jnp.dot(a_ref[...], w_ref[...], preferred_element_type=F32)


def _mm(a, w, lead, n, out_dtype, res=None, tm=1024, tn=1024, tk=None, name="mm"):
    m, kdim = a.shape
    tm, tn = _tile(m, tm), _tile(n, tn, LANE)
    tk = kdim if tk is None else _tile(kdim, tk, LANE)
    nk = kdim // tk
    assert nk == 1 or out_dtype == F32
    nlead = len(lead)
    in_specs = [pl.BlockSpec((tm, tk), lambda i, j, k: (i, k)),
                pl.BlockSpec((None,) * nlead + (tk, tn), lambda i, j, k: lead + (k, j))]
    args = [a, w]
    if res is not None:
        in_specs.append(pl.BlockSpec((tm, tn), lambda i, j, k: (i, j)))
        args.append(res)
    return pl.pallas_call(
        functools.partial(_mm_body, nk=nk, has_res=res is not None),
        grid=(m // tm, n // tn, nk),
        in_specs=in_specs,
        out_specs=pl.BlockSpec((tm, tn), lambda i, j, k: (i, j)),
        out_shape=jax.ShapeDtypeStruct((m, n), out_dtype),
        compiler_params=_cp(("parallel", "parallel", "arbitrary")),
        name=name,
    )(*args)


def _glu_body(*refs, has_gate, blocks_per_expert):
    if has_gate:
        a_ref, wg_ref, wu_ref, gate_ref, o_ref = refs
    else:
        a_ref, wg_ref, wu_ref, o_ref = refs
        gate_ref = None
    a = a_ref[...]
    hg = jnp.dot(a, wg_ref[...], preferred_element_type=F32)
    hu = jnp.dot(a, wu_ref[...], preferred_element_type=F32)
    h = jax.nn.silu(hg) * hu
    if has_gate:
        e = (pl.program_id(1) // blocks_per_expert).astype(F32)
        gates = gate_ref[...]
        lane = lax.broadcasted_iota(jnp.int32, gates.shape, 1).astype(F32)
        h = h * jnp.sum(jnp.where(lane == e, gates, 0.0), axis=-1, keepdims=True)
    o_ref[...] = h.astype(o_ref.dtype)


def _glu_up(a, wg, wu, lead, n_experts, f, gates=None, tm=1024, tn=512):
    m, kdim = a.shape
    tm = _tile(m, tm)
    tn = tn if f % tn == 0 else _tile(f, tn // 2, LANE)
    bpe = f // tn
    nlead = len(lead)
    if gates is not None:
        w_spec = pl.BlockSpec((None,) * (nlead + 1) + (kdim, tn), lambda i, j: lead + (j // bpe, 0, j % bpe))
    else:
        w_spec = pl.BlockSpec((None,) * nlead + (kdim, tn), lambda i, j: lead + (0, j))
    in_specs = [pl.BlockSpec((tm, kdim), lambda i, j: (i, 0)), w_spec, w_spec]
    args = [a, wg, wu]
    if gates is not None:
        in_specs.append(pl.BlockSpec((tm, LANE), lambda i, j: (i, 0)))
        args.append(gates)
    return pl.pallas_call(
        functools.partial(_glu_body, has_gate=gates is not None, blocks_per_expert=bpe),
        grid=(m // tm, n_experts * bpe),
        in_specs=in_specs,
        out_specs=pl.BlockSpec((tm, tn), lambda i, j: (i, j)),
        out_shape=jax.ShapeDtypeStruct((m, n_experts * f), BF16),
        compiler_params=_cp(("parallel", "arbitrary")),
        name="glu_up",
    )(*args)


def _rope_tables(seq_len):
    half = ROPE_DIM // 2
    inv_freq = 1.0 / (ROPE_THETA ** (jnp.arange(half, dtype=F32) * 2.0 / ROPE_DIM))
    ang = jnp.arange(seq_len, dtype=F32)[:, None] * inv_freq[None, :]
    cos, sin = jnp.cos(ang), jnp.sin(ang)
    one = jnp.ones((seq_len, DA_HEAD_DIM - ROPE_DIM), F32)
    zero = jnp.zeros((seq_len, DA_HEAD_DIM - ROPE_DIM), F32)
    zh = jnp.zeros_like(sin)
    comp = lambda a, b, rest: jnp.concatenate([a, b, rest], axis=-1)
    c = comp(cos, cos, one)
    s_from_lower = comp(zh, sin, zero)
    s_from_upper = comp(-sin, zh, zero)
    two = lambda t: jnp.concatenate([t, t], axis=-1)
    return two(c), two(s_from_lower), two(s_from_upper)


def _rope_body(x_ref, c_ref, sl_ref, su_ref, o_ref, *, q_scale):
    c, sl, su = c_ref[...], sl_ref[...], su_ref[...]
    n_blocks = 2 * DA_HEADS
    for j in range(n_blocks):
        t = x_ref[:, j * LANE:(j + 1) * LANE].astype(F32)
        r = t * c + pltpu.roll(t, ROPE_DIM // 2, 1) * sl + pltpu.roll(t, LANE - ROPE_DIM // 2, 1) * su
        if j < DA_HEADS:
            r = r * q_scale
        o_ref[:, j * LANE:(j + 1) * LANE] = r.astype(o_ref.dtype)


def _rope(u, row0, nb, seq_len, tables):
    tl = _tile(seq_len, 512, 16)
    nl = seq_len // tl
    r0 = row0 // tl
    width = 2 * MIX
    tab_spec = pl.BlockSpec((tl, LANE), lambda i: (i % nl, 0))
    return pl.pallas_call(
        functools.partial(_rope_body, q_scale=DA_HEAD_DIM ** -0.5),
        grid=(nb * nl,),
        in_specs=[pl.BlockSpec((tl, width), lambda i: (r0 + i, 0)), tab_spec, tab_spec, tab_spec],
        out_specs=pl.BlockSpec((tl, width), lambda i: (i, 0)),
        out_shape=jax.ShapeDtypeStruct((nb * seq_len, width), BF16),
        compiler_params=_cp(("parallel",)),
        name="rope",
    )(u, *tables)


def _attn_body(lam_ref, g_ref, q_ref, k_ref, v_ref, o_ref, m_ref, l_ref, acc_ref, *, tk, nkv, post_scale):
    q = q_ref[...]
    lane = lax.broadcasted_iota(jnp.int32, q.shape, 1)
    zero = jnp.zeros_like(q)
    qs = (jnp.where(lane < DA_HEAD_DIM, q, zero), jnp.where(lane >= DA_HEAD_DIM, q, zero))
    nt = tk // LANE
    scores = lambda c, kk: lax.dot_general(qs[c], kk, (((1,), (1,)), ((), ())), preferred_element_type=F32)

    m_ref[...] = jnp.full(m_ref.shape, -jnp.inf, F32)

    def max_step(j, carry):
        kk = k_ref[pl.ds(pl.multiple_of(j * tk, tk), tk), :]
        for c in range(2):
            s = scores(c, kk)
            part = s[:, :LANE]
            for t in range(1, nt):
                part = jnp.maximum(part, s[:, t * LANE:(t + 1) * LANE])
            m_ref[c] = jnp.maximum(m_ref[c], part)
        return carry

    lax.fori_loop(0, nkv, max_step, 0)
    for c in range(2):
        m_ref[c] = jnp.broadcast_to(jnp.max(m_ref[c], axis=-1, keepdims=True), m_ref.shape[1:])

    l_ref[...] = jnp.zeros(l_ref.shape, F32)
    acc_ref[...] = jnp.zeros(acc_ref.shape, F32)

    def step(j, carry):
        start = pl.multiple_of(j * tk, tk)
        kk = k_ref[pl.ds(start, tk), :]
        vv = v_ref[pl.ds(start, tk), :]
        for c in range(2):
            s = scores(c, kk)
            m = m_ref[c]
            ps = [jnp.exp(s[:, t * LANE:(t + 1) * LANE] - m) for t in range(nt)]
            l_ref[c] += functools.reduce(lambda a, b: a + b, ps)
            p = jnp.concatenate(ps, axis=1).astype(BF16)
            acc_ref[c] += jnp.dot(p, vv, preferred_element_type=F32)
        return carry

    lax.fori_loop(0, nkv, step, 0)
    l0 = jnp.sum(l_ref[0], axis=-1, keepdims=True)
    l1 = jnp.sum(l_ref[1], axis=-1, keepdims=True)
    o = acc_ref[0] / l0 - lam_ref[...] * (acc_ref[1] / l1)
    o_ref[...] = (_rms(o, g_ref[...]) * post_scale).astype(o_ref.dtype)


def _attention(qk, u, row0, nb, seq_len, lam, g_subln, lam_init):
    tq = _tile(seq_len, 1024, 16)
    tk = _tile(seq_len, 1024, LANE)
    nq = seq_len // tq
    assert row0 % seq_len == 0
    rb = row0 // seq_len
    return pl.pallas_call(
        functools.partial(_attn_body, tk=tk, nkv=seq_len // tk, post_scale=1.0 - lam_init),
        grid=(nb, DA_HEADS, nq),
        in_specs=[pl.BlockSpec((1, LANE), lambda b, h, i: (0, 0)),
                  pl.BlockSpec((1, LANE), lambda b, h, i: (0, 0)),
                  pl.BlockSpec((tq, LANE), lambda b, h, i: (b * nq + i, h)),
                  pl.BlockSpec((seq_len, LANE), lambda b, h, i: (b, DA_HEADS + h)),
                  pl.BlockSpec((seq_len, LANE), lambda b, h, i: (rb + b, COL_V // LANE + h))],
        out_specs=pl.BlockSpec((tq, LANE), lambda b, h, i: (b * nq + i, h)),
        out_shape=jax.ShapeDtypeStruct((nb * seq_len, MIX), BF16),
        scratch_shapes=[pltpu.VMEM((2, tq, LANE), F32), pltpu.VMEM((2, tq, LANE), F32),
                        pltpu.VMEM((2, tq, DA_V_DIM), F32)],
        compiler_params=_cp(("parallel", "parallel", "arbitrary")),
        name="diff_attention",
    )(jnp.full((1, LANE), lam, F32), g_subln.reshape(1, DA_V_DIM).astype(F32), qk, qk, u)


def _dft_cs(n):
    idx = jnp.arange(n, dtype=jnp.int32)
    ang = ((idx[:, None] * idx[None, :]) % n).astype(F32) * (2.0 * math.pi / n)
    return jnp.cos(ang), jnp.sin(ang)


def _fourier_consts(seq_len):
    n1 = 1 << ((seq_len.bit_length() - 1 + 1) // 2)
    n2 = seq_len // n1
    assert n1 * n2 == seq_len and n1 % 16 == 0 and n2 % 16 == 0
    cc, sc = _dft_cs(FN_GROUP_DIM)
    w_chan = jnp.concatenate([cc, -sc], axis=1).astype(BF16)
    c1, s1 = _dft_cs(n1)
    m1 = jnp.concatenate([jnp.concatenate([c1, s1], 1), jnp.concatenate([-s1, c1], 1)], 0).astype(BF16)
    k1 = jnp.arange(n1, dtype=jnp.int32)[:, None]
    i2 = jnp.arange(n2, dtype=jnp.int32)[None, :]
    ang = ((k1 * i2) % seq_len).astype(F32) * (2.0 * math.pi / seq_len)
    expand = lambda t: jnp.repeat(t, FN_GROUP_DIM, axis=1)
    c2, s2 = _dft_cs(n2)
    return dict(n1=n1, n2=n2, w_chan=w_chan, m1=m1, tw_c=expand(jnp.cos(ang)), tw_s=expand(jnp.sin(ang)),
                c2=c2.astype(BF16), s2=s2.astype(BF16))


def _fn_chan_body(x_ref, w_ref, o_ref):
    y = jnp.dot(x_ref[...], w_ref[...], preferred_element_type=F32)
    o_ref[0] = y[:, :FN_GROUP_DIM].astype(o_ref.dtype)
    o_ref[1] = y[:, FN_GROUP_DIM:].astype(o_ref.dtype)


def _fn_stage1_body(y_ref, m1_ref, tc_ref, ts_ref, o_ref, *, n1):
    a = jnp.dot(m1_ref[...], y_ref[...], preferred_element_type=F32)
    ar, ai = a[:n1], a[n1:]
    tc, ts = tc_ref[...], ts_ref[...]
    o_ref[:n1, :] = (ar * tc + ai * ts).astype(o_ref.dtype)
    o_ref[n1:, :] = (ai * tc - ar * ts).astype(o_ref.dtype)


def _fn_stage2_body(a_ref, c2_ref, s2_ref, o_ref, *, kb, norm):
    c2, s2 = c2_ref[...], s2_ref[...]
    for j in range(kb):
        for g in range(FN_GROUPS):
            r = (jnp.dot(c2, a_ref[g, 0, j], preferred_element_type=F32)
                 + jnp.dot(s2, a_ref[g, 1, j], preferred_element_type=F32))
            col = (j * FN_GROUPS + g) * FN_GROUP_DIM
            o_ref[:, col:col + FN_GROUP_DIM] = (r * norm).astype(o_ref.dtype)


def _fourier_mix(u, row0, nb, seq_len, fc):
    n1, n2 = fc["n1"], fc["n2"]
    gd = FN_GROUP_DIM
    tl = _tile(seq_len, 512, 16)
    nl = seq_len // tl
    r0 = row0 // tl
    y = pl.pallas_call(
        _fn_chan_body,
        grid=(nb, FN_GROUPS, nl),
        in_specs=[pl.BlockSpec((tl, gd), lambda b, g, i: (r0 + b * nl + i, COL_FN // gd + g)),
                  pl.BlockSpec((gd, 2 * gd), lambda b, g, i: (0, 0))],
        out_specs=pl.BlockSpec((None, None, 2, tl, gd), lambda b, g, i: (b, g, 0, i, 0)),
        out_shape=jax.ShapeDtypeStruct((nb, FN_GROUPS, 2, seq_len, gd), BF16),
        compiler_params=_cp(("parallel", "parallel", "parallel")),
        name="fnet_channel_dft",
    )(u, fc["w_chan"])

    cols = n2 * gd
    tc = _tile(cols, 2048, LANE)
    a = pl.pallas_call(
        functools.partial(_fn_stage1_body, n1=n1),
        grid=(nb, FN_GROUPS, cols // tc),
        in_specs=[pl.BlockSpec((None, None, 2 * n1, tc), lambda b, g, i: (b, g, 0, i)),
                  pl.BlockSpec((2 * n1, 2 * n1), lambda b, g, i: (0, 0)),
                  pl.BlockSpec((n1, tc), lambda b, g, i: (0, i)),
                  pl.BlockSpec((n1, tc), lambda b, g, i: (0, i))],
        out_specs=pl.BlockSpec((None, None, 2 * n1, tc), lambda b, g, i: (b, g, 0, i)),
        out_shape=jax.ShapeDtypeStruct((nb, FN_GROUPS, 2 * n1, cols), BF16),
        compiler_params=_cp(("parallel", "parallel", "parallel")),
        name="fnet_seq_dft_stage1",
    )(y.reshape(nb, FN_GROUPS, 2 * n1, cols), fc["m1"], fc["tw_c"], fc["tw_s"])

    kb = _tile(n1, 8, 1)
    out = pl.pallas_call(
        functools.partial(_fn_stage2_body, kb=kb, norm=1.0 / math.sqrt(seq_len * gd)),
        grid=(nb, n1 // kb),
        in_specs=[pl.BlockSpec((None, FN_GROUPS, 2, kb, n2, gd), lambda b, i: (b, 0, 0, i, 0, 0)),
                  pl.BlockSpec((n2, n2), lambda b, i: (0, 0)),
                  pl.BlockSpec((n2, n2), lambda b, i: (0, 0))],
        out_specs=pl.BlockSpec((None, n2, kb * MIX), lambda b, i: (b, 0, i)),
        out_shape=jax.ShapeDtypeStruct((nb, n2, n1 * MIX), BF16),
        compiler_params=_cp(("parallel", "parallel")),
        name="fnet_seq_dft_stage2",
    )(a.reshape(nb, FN_GROUPS, 2, n1, n2, gd), fc["c2"], fc["s2"])
    return out.reshape(nb * seq_len, MIX)


def _conv_body(ap_ref, ac_ref, an_ref, gp_ref, gc_ref, gn_ref, w_ref, b_ref, lg_ref, lb_ref, o_ref, z_ref,
               *, tl, nl):
    i = pl.program_id(1)
    glu = lambda a_ref, g_ref: a_ref[...].astype(F32) * jax.nn.sigmoid(g_ref[...].astype(F32))
    z_ref[0:CONV_HALO, :] = jnp.where(i > 0, glu(ap_ref, gp_ref), 0.0)
    z_ref[CONV_HALO:CONV_HALO + tl, :] = glu(ac_ref, gc_ref)
    z_ref[CONV_HALO + tl:2 * CONV_HALO + tl, :] = jnp.where(i < nl - 1, glu(an_ref, gn_ref), 0.0)
    acc = jnp.broadcast_to(b_ref[...], (tl, MIX))
    first = CONV_HALO - CONV_TAPS // 2
    for j in range(CONV_TAPS):
        acc = acc + w_ref[j:j + 1, :] * z_ref[first + j:first + j + tl, :]
    mu = jnp.mean(acc, axis=-1, keepdims=True)
    xc = acc - mu
    var = jnp.mean(xc * xc, axis=-1, keepdims=True)
    y = xc * lax.rsqrt(var + LN_EPS) * lg_ref[...] + lb_ref[...]
    o_ref[...] = jax.nn.silu(y).astype(o_ref.dtype)


def _conformer_conv(u, row0, nb, seq_len, w_dw, b_dw, ln_g, ln_b):
    m = u.shape[0]
    tl = _tile(seq_len, 256, CONV_HALO)
    nl = seq_len // tl
    r0 = row0 // tl
    hb = tl // CONV_HALO
    last_hb = m // CONV_HALO - 1
    ca, cg = COL_CV // MIX, COL_CV // MIX + 1

    def cur(c):
        return pl.BlockSpec((tl, MIX), lambda b, i: (r0 + b * nl + i, c))

    def prev(c):
        return pl.BlockSpec((CONV_HALO, MIX), lambda b, i: (jnp.maximum((r0 + b * nl + i) * hb - 1, 0), c))

    def nxt(c):
        return pl.BlockSpec((CONV_HALO, MIX), lambda b, i: (jnp.minimum((r0 + b * nl + i + 1) * hb, last_hb), c))

    vec = pl.BlockSpec((1, MIX), lambda b, i: (0, 0))
    return pl.pallas_call(
        functools.partial(_conv_body, tl=tl, nl=nl),
        grid=(nb, nl),
        in_specs=[prev(ca), cur(ca), nxt(ca), prev(cg), cur(cg), nxt(cg),
                  pl.BlockSpec((CONV_TAPS, MIX), lambda b, i: (0, 0)), vec, vec, vec],
        out_specs=pl.BlockSpec((tl, MIX), lambda b, i: (b * nl + i, 0)),
        out_shape=jax.ShapeDtypeStruct((nb * seq_len, MIX), BF16),
        scratch_shapes=[pltpu.VMEM((tl + 2 * CONV_HALO, MIX), F32)],
        compiler_params=_cp(("parallel", "arbitrary")),
        name="conformer_conv",
    )(u, u, u, u, u, u, w_dw, b_dw.reshape(1, MIX), ln_g.reshape(1, MIX), ln_b.reshape(1, MIX))


def _log_sigmoid(z):
    return jnp.minimum(z, 0.0) - jnp.log1p(jnp.exp(-jnp.abs(z)))


def _gla_scan_body(qf_ref, kf_ref, vf_ref, gaf_ref, qb_ref, kb_ref, vb_ref, gab_ref, w2_ref, b2_ref,
                   of_ref, ob_ref, st_ref):
    @pl.when(pl.program_id(1) == 0)
    def _():
        st_ref[...] = jnp.zeros(st_ref.shape, F32)

    c = GLA_CHUNK
    row = lax.broadcasted_iota(jnp.int32, (c, c), 0)
    col = lax.broadcasted_iota(jnp.int32, (c, c), 1)
    kw = GLA_HEADS * GLA_DK
    dirs = ((qf_ref, kf_ref, vf_ref, gaf_ref, of_ref, col <= row),
            (qb_ref, kb_ref, vb_ref, gab_ref, ob_ref, col >= row))
    for d, (q_ref, k_ref, v_ref, ga_ref, o_ref, keep) in enumerate(dirs):
        z = jnp.dot(ga_ref[...], w2_ref[:, d * kw:(d + 1) * kw], preferred_element_type=F32)
        g_all = _log_sigmoid(z + b2_ref[:, d * kw:(d + 1) * kw]) / GLA_TAU
        tri = keep.astype(F32)
        for h in range(GLA_HEADS):
            ks = slice(h * GLA_DK, (h + 1) * GLA_DK)
            vs = slice(h * GLA_DV, (h + 1) * GLA_DV)
            g = g_all[:, ks]
            b = jnp.dot(tri, g, preferred_element_type=F32, precision=lax.Precision.HIGHEST)
            b_tot = jnp.sum(g, axis=0, keepdims=True)
            q = q_ref[:, ks].astype(F32) * (GLA_DK ** -0.5)
            k = k_ref[:, ks].astype(F32)
            v = v_ref[:, vs]
            q_dec = (q * jnp.exp(b)).astype(BF16)
            k_inv = (k * jnp.exp(-b)).astype(BF16)
            k_end = (k * jnp.exp(b_tot - b)).astype(BF16)
            st = st_ref[d, h]
            o_inter = lax.dot_general(q_dec, st.astype(BF16), (((1,), (1,)), ((), ())),
                                      preferred_element_type=F32)
            a = lax.dot_general(q_dec, k_inv, (((1,), (1,)), ((), ())), preferred_element_type=F32)
            a = jnp.where(keep, a, 0.0)
            o_intra = jnp.dot(a.astype(BF16), v, preferred_element_type=F32)
            o_ref[:, vs] = o_inter + o_intra
            st_ref[d, h] = st * jnp.exp(b_tot) + lax.dot_general(
                v, k_end, (((0,), (0,)), ((), ())), preferred_element_type=F32)


def _gla_post_body(of_ref, ob_ref, r_ref, g_ref, o_ref):
    g = g_ref[...]
    for h in range(GLA_HEADS):
        vs = slice(h * GLA_DV, (h + 1) * GLA_DV)
        o = _rms(of_ref[:, vs] + ob_ref[:, vs], g)
        o_ref[:, vs] = (o * jax.nn.silu(r_ref[:, vs].astype(F32))).astype(o_ref.dtype)


def _gla(u, ga, row0, nb, seq_len, w2, b2, g_norm):
    c = GLA_CHUNK
    n = seq_len // c
    r0 = row0 // c
    kw = GLA_HEADS * GLA_DK
    fwd = lambda col: (lambda b, i: (r0 + b * n + i, col))
    bwd = lambda col: (lambda b, i: (r0 + b * n + n - 1 - i, col))
    specs = []
    for imap in (fwd, bwd):
        specs += [pl.BlockSpec((c, kw), imap(COL_GQ // kw)), pl.BlockSpec((c, kw), imap(COL_GK // kw)),
                  pl.BlockSpec((c, MIX), imap(COL_GV // MIX)), pl.BlockSpec((c, LANE), imap(0))]
    specs += [pl.BlockSpec((LANE, 2 * kw), lambda b, i: (0, 0)), pl.BlockSpec((1, 2 * kw), lambda b, i: (0, 0))]
    o_shape = jax.ShapeDtypeStruct((nb * seq_len, MIX), F32)
    o_f, o_b = pl.pallas_call(
        _gla_scan_body,
        grid=(nb, n),
        in_specs=specs,
        out_specs=[pl.BlockSpec((c, MIX), lambda b, i: (b * n + i, 0)),
                   pl.BlockSpec((c, MIX), lambda b, i: (b * n + n - 1 - i, 0))],
        out_shape=[o_shape, o_shape],
        scratch_shapes=[pltpu.VMEM((2, GLA_HEADS, GLA_DV, GLA_DK), F32)],
        compiler_params=_cp(("parallel", "arbitrary")),
        name="gla_scan",
    )(u, u, u, ga, u, u, u, ga, w2, b2)

    tl = _tile(seq_len, 256, 16)
    nl = seq_len // tl
    rr = row0 // tl
    row_spec = pl.BlockSpec((tl, MIX), lambda i: (i, 0))
    return pl.pallas_call(
        _gla_post_body,
        grid=(nb * nl,),
        in_specs=[row_spec, row_spec, pl.BlockSpec((tl, MIX), lambda i: (rr + i, COL_GR // MIX)),
                  pl.BlockSpec((1, GLA_DV), lambda i: (0, 0))],
        out_specs=row_spec,
        out_shape=jax.ShapeDtypeStruct((nb * seq_len, MIX), BF16),
        compiler_params=_cp(("parallel",)),
        name="gla_post",
    )(o_f, o_b, u, g_norm.reshape(1, GLA_DV))


def _merge_body(h_ref, wg_ref, bg_ref, z_ref, wb_ref, o_ref, acc_ref, *, n_branch):
    i = pl.program_id(2)
    gate = jax.nn.sigmoid(jnp.dot(h_ref[...], wg_ref[...], preferred_element_type=F32) + bg_ref[...])
    contrib = gate * jnp.dot(z_ref[...], wb_ref[...], preferred_element_type=F32)

    @pl.when(i == 0)
    def _():
        acc_ref[...] = contrib

    @pl.when(i > 0)
    def _():
        acc_ref[...] += contrib

    @pl.when(i == n_branch - 1)
    def _():
        o_ref[...] = acc_ref[...].astype(o_ref.dtype)


def _merge(h, z, w_gate, b_gate, w_branch, layer, tm=1024, tn=512):
    m, d = h.shape
    n_branch = z.shape[0]
    tm, tn = _tile(m, tm), _tile(d, tn, LANE)
    return pl.pallas_call(
        functools.partial(_merge_body, n_branch=n_branch),
        grid=(m // tm, d // tn, n_branch),
        in_specs=[pl.BlockSpec((tm, d), lambda a, j, i: (a, 0)),
                  pl.BlockSpec((None, None, d, tn), lambda a, j, i: (layer, i, 0, j)),
                  pl.BlockSpec((None, None, 1, tn), lambda a, j, i: (layer, i, 0, j)),
                  pl.BlockSpec((None, tm, MIX), lambda a, j, i: (i, a, 0)),
                  pl.BlockSpec((None, None, MIX, tn), lambda a, j, i: (layer, i, 0, j))],
        out_specs=pl.BlockSpec((tm, tn), lambda a, j, i: (a, j)),
        out_shape=jax.ShapeDtypeStruct((m, d), BF16),
        scratch_shapes=[pltpu.VMEM((tm, tn), F32)],
        compiler_params=_cp(("parallel", "parallel", "arbitrary")),
        name="gated_merge",
    )(h, w_gate, b_gate.reshape(b_gate.shape[0], n_branch, 1, d), z, w_branch)


def _router_body(x_ref, g_ref, wr_ref, h_ref, gate_ref, *, n_experts):
    hb = _rms(x_ref[...], g_ref[...]).astype(BF16)
    h_ref[...] = hb
    logits = jnp.dot(hb, wr_ref[...], preferred_element_type=F32)
    lane = lax.broadcasted_iota(jnp.int32, logits.shape, 1).astype(F32)
    neg = jnp.float32(-jnp.inf)
    big = jnp.float32(LANE)
    lg = jnp.where(lane < n_experts, logits, neg)
    m1 = jnp.max(lg, axis=-1, keepdims=True)
    i1 = jnp.min(jnp.where(lg == m1, lane, big), axis=-1, keepdims=True)
    lg2 = jnp.where(lane == i1, neg, lg)
    m2 = jnp.max(lg2, axis=-1, keepdims=True)
    i2 = jnp.min(jnp.where(lg2 == m2, lane, big), axis=-1, keepdims=True)
    e2 = jnp.exp(m2 - m1)
    denom = 1.0 + e2
    gate_ref[...] = jnp.where(lane == i1, 1.0 / denom, 0.0) + jnp.where(lane == i2, e2 / denom, 0.0)


def _norm_router(x, g, w_router_pad, n_experts):
    m, d = x.shape
    tm = _tile(m, 256)
    return pl.pallas_call(
        functools.partial(_router_body, n_experts=n_experts),
        grid=(m // tm,),
        in_specs=[pl.BlockSpec((tm, d), lambda i: (i, 0)),
                  pl.BlockSpec((1, d), lambda i: (0, 0)),
                  pl.BlockSpec((d, LANE), lambda i: (0, 0))],
        out_specs=[pl.BlockSpec((tm, d), lambda i: (i, 0)), pl.BlockSpec((tm, LANE), lambda i: (i, 0))],
        out_shape=[jax.ShapeDtypeStruct((m, d), BF16), jax.ShapeDtypeStruct((m, LANE), F32)],
        compiler_params=_cp(("parallel",)),
        name="norm_router",
    )(x, g.reshape(1, d), w_router_pad)


def kernel(x_prompt, x_sample, norm_mix, w_in, da_lambda, da_subln_g, cv_dw_w, cv_dw_b, cv_ln_g, cv_ln_b,
           gla_w_a2, gla_b_a2, gla_norm_g, w_gate, b_gate, w_branch, w_out, norm_ffn, ffn_w_gate, ffn_w_up,
           ffn_w_down, moe_router, moe_w_gate, moe_w_up, moe_w_down, norm_final):
    depth = norm_mix.shape[0]
    d = x_prompt.shape[-1]
    seqs = []
    row = 0
    for xs in (x_prompt, x_sample):
        seqs.append((row, xs.shape[0], xs.shape[1]))
        row += xs.shape[0] * xs.shape[1]
    x = jnp.concatenate([x_prompt.reshape(-1, d), x_sample.reshape(-1, d)], axis=0)

    w_in_b = w_in.astype(BF16)
    w_ga_b = jnp.pad(w_in[:, :, COL_GA:], ((0, 0), (0, 0), (0, LANE - 2 * GLA_RANK))).astype(BF16)
    w_gate_b = w_gate.astype(BF16)
    w_branch_b = w_branch.astype(BF16)
    w_out_b = w_out.astype(BF16)
    d_ff = ffn_w_gate.shape[-1]
    f_pad = _round_up(d_ff, 1024 if d_ff >= 1024 else LANE) - d_ff
    ffn_wg_b = jnp.pad(ffn_w_gate, ((0, 0), (0, 0), (0, f_pad))).astype(BF16)
    ffn_wu_b = jnp.pad(ffn_w_up, ((0, 0), (0, 0), (0, f_pad))).astype(BF16)
    ffn_wd_b = jnp.pad(ffn_w_down, ((0, 0), (0, f_pad), (0, 0))).astype(BF16)
    n_experts, f_exp = moe_w_gate.shape[1], moe_w_gate.shape[-1]
    moe_wg_b = moe_w_gate.astype(BF16)
    moe_wu_b = moe_w_up.astype(BF16)
    moe_wd_b = moe_w_down.astype(BF16).reshape(moe_w_down.shape[0], n_experts * f_exp, d)
    router_b = jnp.pad(moe_router, ((0, 0), (0, 0), (0, LANE - n_experts))).astype(BF16)
    kw = GLA_HEADS * GLA_DK
    w2 = jnp.zeros((depth, LANE, 2 * kw), F32)
    w2 = w2.at[:, :GLA_RANK, :kw].set(gla_w_a2[:, 0]).at[:, GLA_RANK:2 * GLA_RANK, kw:].set(gla_w_a2[:, 1])
    w2 = w2.astype(BF16)
    b2 = gla_b_a2.reshape(depth, 1, 2 * kw)

    rope_tabs = {s[2]: _rope_tables(s[2]) for s in seqs}
    fn_consts = {s[2]: _fourier_consts(s[2]) for s in seqs}

    for l in range(depth):
        h = _rmsnorm(x, norm_mix[l], BF16)
        u = _mm(h, w_in_b, (l,), COL_GA, BF16, name="in_proj")
        ga = _mm(h, w_ga_b, (l,), LANE, BF16, name="gla_gate_proj")
        lam_init = 0.8 - 0.6 * math.exp(-0.3 * l)
        lp = da_lambda[l].astype(F32)
        lam = jnp.exp(jnp.sum(lp[0] * lp[1])) - jnp.exp(jnp.sum(lp[2] * lp[3])) + lam_init
        zs = [[], [], [], []]
        for row0, nb, seq_len in seqs:
            qk = _rope(u, row0, nb, seq_len, rope_tabs[seq_len])
            zs[0].append(_attention(qk, u, row0, nb, seq_len, lam, da_subln_g[l], lam_init))
            zs[1].append(_fourier_mix(u, row0, nb, seq_len, fn_consts[seq_len]))
            zs[2].append(_conformer_conv(u, row0, nb, seq_len, cv_dw_w[l], cv_dw_b[l], cv_ln_g[l], cv_ln_b[l]))
            zs[3].append(_gla(u, ga, row0, nb, seq_len, w2[l], b2[l], gla_norm_g[l]))
        z = jnp.stack([jnp.concatenate(parts, axis=0) for parts in zs], axis=0)
        merged = _merge(h, z, w_gate_b, b_gate, w_branch_b, l)
        x = _mm(merged, w_out_b, (l,), d, F32, res=x, tn=512, name="out_proj")
        j = l // 2
        if l % 2 == 0:
            h2 = _rmsnorm(x, norm_ffn[l], BF16)
            mid = _glu_up(h2, ffn_wg_b, ffn_wu_b, (j,), 1, d_ff + f_pad)
            x = _mm(mid, ffn_wd_b, (j,), d, F32, res=x, tk=(d_ff + f_pad) // 4, name="ffn_down")
        else:
            h2, gates = _norm_router(x, norm_ffn[l], router_b[j], n_experts)
            mid = _glu_up(h2, moe_wg_b, moe_wu_b, (j,), n_experts, f_exp, gates=gates)
            x = _mm(mid, moe_wd_b, (j,), d, F32, res=x, tk=n_experts * f_exp // 4, name="moe_down")
    y = _rmsnorm(x, norm_final, F32)
    n_prompt = seqs[0][1] * seqs[0][2]
    return y[:n_prompt].reshape(x_prompt.shape), y[n_prompt:].reshape(x_sample.shape)
```

```python
import functools
import math

import jax
import jax.numpy as jnp
from jax import lax
from jax.experimental import pallas as pl
from jax.experimental.pallas import tpu as pltpu

F32 = jnp.float32
BF16 = jnp.bfloat16

MIX = 1024
DA_HEADS = 8
DA_HEAD_DIM = 64
DA_V_DIM = 128
ROPE_DIM = 16
ROPE_THETA = 500000.0
FN_GROUPS = 4
FN_GROUP_DIM = 256
CONV_TAPS = 31
CONV_HALO = 16
GLA_HEADS = 4
GLA_DK = 128
GLA_DV = 256
GLA_RANK = 16
GLA_TAU = 16.0
GLA_CHUNK = 64
TOP_K = 2
RMS_EPS = 1e-6
LN_EPS = 1e-5

COL_Q = 0
COL_K = 1024
COL_V = 2048
COL_FN = 3072
COL_CV = 4096
COL_GQ = 6144
COL_GK = 6656
COL_GV = 7168
COL_GR = 8192
COL_GA = 9216
LANE = 128

VMEM_LIMIT_MB = 56


def _cp(sem, vmem_mb=VMEM_LIMIT_MB):
    return pltpu.CompilerParams(dimension_semantics=sem, vmem_limit_bytes=vmem_mb * 2**20)


def _tile(n, pref, align=8):
    if n <= pref:
        return n
    t = (pref // align) * align
    while t >= align:
        if n % t == 0:
            return t
        t -= align
    raise ValueError(f"no tile for {n} (pref {pref}, align {align})")


def _round_up(n, m):
    return (n + m - 1) // m * m


def _rms(x, g):
    ms = jnp.mean(x * x, axis=-1, keepdims=True)
    return x * lax.rsqrt(ms + RMS_EPS) * g


def _rmsnorm_body(x_ref, g_ref, o_ref):
    o_ref[...] = _rms(x_ref[...], g_ref[...]).astype(o_ref.dtype)


def _rmsnorm(x, g, out_dtype, row0=0, rows=None):
    d = x.shape[1]
    m = x.shape[0] if rows is None else rows
    tm = math.gcd(_tile(m, 256), row0) if row0 else _tile(m, 256)
    r0 = row0 // tm
    return pl.pallas_call(
        _rmsnorm_body,
        grid=(m // tm,),
        in_specs=[pl.BlockSpec((tm, d), lambda i: (r0 + i, 0)),
                  pl.BlockSpec((1, d), lambda i: (0, 0))],
        out_specs=pl.BlockSpec((tm, d), lambda i: (i, 0)),
        out_shape=jax.ShapeDtypeStruct((m, d), out_dtype),
        compiler_params=_cp(("parallel",)),
        name="rmsnorm",
    )(x, g.reshape(1, d))


def _mm_body(*refs, nk, has_res):
    if has_res:
        a_ref, w_ref, r_ref, o_ref = refs
    else:
        a_ref, w_ref, o_ref = refs
        r_ref = None

    def first():
        r = jnp.dot(a_ref[...], w_ref[...], preferred_element_type=F32)
        if has_res:
            r = r + r_ref[...]
        o_ref[...] = r.astype(o_ref.dtype)

    if nk == 1:
        first()
        return
    k = pl.program_id(2)
    pl.when(k == 0)(first)

    @pl.when(k > 0)
    def _():
        o_ref[...] += jnp.dot(a_ref[...], w_ref[...], preferred_element_type=F32)


def _mm(a, w, lead, n, out_dtype, res=None, tm=1024, tn=1024, tk=None, name="mm"):
    m, kdim = a.shape
    tm, tn = _tile(m, tm), _tile(n, tn, LANE)
    tk = kdim if tk is None else _tile(kdim, tk, LANE)
    nk = kdim // tk
    assert nk == 1 or out_dtype == F32
    nlead = len(lead)
    in_specs = [pl.BlockSpec((tm, tk), lambda i, j, k: (i, k)),
                pl.BlockSpec((None,) * nlead + (tk, tn), lambda i, j, k: lead + (k, j))]
    args = [a, w]
    if res is not None:
        in_specs.append(pl.BlockSpec((tm, tn), lambda i, j, k: (i, j)))
        args.append(res)
    return pl.pallas_call(
        functools.partial(_mm_body, nk=nk, has_res=res is not None),
        grid=(m // tm, n // tn, nk),
        in_specs=in_specs,
        out_specs=pl.BlockSpec((tm, tn), lambda i, j, k: (i, j)),
        out_shape=jax.ShapeDtypeStruct((m, n), out_dtype),
        compiler_params=_cp(("parallel", "parallel", "arbitrary")),
        name=name,
    )(*args)


def _glu_body(*refs, has_gate, blocks_per_expert):
    if has_gate:
        a_ref, wg_ref, wu_ref, gate_ref, o_ref = refs
    else:
        a_ref, wg_ref, wu_ref, o_ref = refs
        gate_ref = None
    a = a_ref[...]
    hg = jnp.dot(a, wg_ref[...], preferred_element_type=F32)
    hu = jnp.dot(a, wu_ref[...], preferred_element_type=F32)
    h = jax.nn.silu(hg) * hu
    if has_gate:
        e = (pl.program_id(1) // blocks_per_expert).astype(F32)
        gates = gate_ref[...]
        lane = lax.broadcasted_iota(jnp.int32, gates.shape, 1).astype(F32)
        h = h * jnp.sum(jnp.where(lane == e, gates, 0.0), axis=-1, keepdims=True)
    o_ref[...] = h.astype(o_ref.dtype)


def _glu_up(a, wg, wu, lead, n_experts, f, gates=None, tm=1024, tn=512):
    m, kdim = a.shape
    tm = _tile(m, tm)
    tn = tn if f % tn == 0 else _tile(f, tn // 2, LANE)
    bpe = f // tn
    nlead = len(lead)
    if gates is not None:
        w_spec = pl.BlockSpec((None,) * (nlead + 1) + (kdim, tn), lambda i, j: lead + (j // bpe, 0, j % bpe))
    else:
        w_spec = pl.BlockSpec((None,) * nlead + (kdim, tn), lambda i, j: lead + (0, j))
    in_specs = [pl.BlockSpec((tm, kdim), lambda i, j: (i, 0)), w_spec, w_spec]
    args = [a, wg, wu]
    if gates is not None:
        in_specs.append(pl.BlockSpec((tm, LANE), lambda i, j: (i, 0)))
        args.append(gates)
    return pl.pallas_call(
        functools.partial(_glu_body, has_gate=gates is not None, blocks_per_expert=bpe),
        grid=(m // tm, n_experts * bpe),
        in_specs=in_specs,
        out_specs=pl.BlockSpec((tm, tn), lambda i, j: (i, j)),
        out_shape=jax.ShapeDtypeStruct((m, n_experts * f), BF16),
        compiler_params=_cp(("parallel", "arbitrary")),
        name="glu_up",
    )(*args)


def _rope_tables(seq_len):
    half = ROPE_DIM // 2
    inv_freq = 1.0 / (ROPE_THETA ** (jnp.arange(half, dtype=F32) * 2.0 / ROPE_DIM))
    ang = jnp.arange(seq_len, dtype=F32)[:, None] * inv_freq[None, :]
    cos, sin = jnp.cos(ang), jnp.sin(ang)
    one = jnp.ones((seq_len, DA_HEAD_DIM - ROPE_DIM), F32)
    zero = jnp.zeros((seq_len, DA_HEAD_DIM - ROPE_DIM), F32)
    zh = jnp.zeros_like(sin)
    comp = lambda a, b, rest: jnp.concatenate([a, b, rest], axis=-1)
    c = comp(cos, cos, one)
    s_from_lower = comp(zh, sin, zero)
    s_from_upper = comp(-sin, zh, zero)
    two = lambda t: jnp.concatenate([t, t], axis=-1)
    return two(c), two(s_from_lower), two(s_from_upper)


def _rope_body(x_ref, c_ref, sl_ref, su_ref, o_ref, *, q_scale):
    c, sl, su = c_ref[...], sl_ref[...], su_ref[...]
    n_blocks = 2 * DA_HEADS
    for j in range(n_blocks):
        t = x_ref[:, j * LANE:(j + 1) * LANE].astype(F32)
        r = t * c + pltpu.roll(t, ROPE_DIM // 2, 1) * sl + pltpu.roll(t, LANE - ROPE_DIM // 2, 1) * su
        if j < DA_HEADS:
            r = r * q_scale
        o_ref[:, j * LANE:(j + 1) * LANE] = r.astype(o_ref.dtype)


def _rope(u, row0, nb, seq_len, tables):
    tl = _tile(seq_len, 512, 16)
    nl = seq_len // tl
    r0 = row0 // tl
    width = 2 * MIX
    tab_spec = pl.BlockSpec((tl, LANE), lambda i: (i % nl, 0))
    return pl.pallas_call(
        functools.partial(_rope_body, q_scale=DA_HEAD_DIM ** -0.5),
        grid=(nb * nl,),
        in_specs=[pl.BlockSpec((tl, width), lambda i: (r0 + i, 0)), tab_spec, tab_spec, tab_spec],
        out_specs=pl.BlockSpec((tl, width), lambda i: (i, 0)),
        out_shape=jax.ShapeDtypeStruct((nb * seq_len, width), BF16),
        compiler_params=_cp(("parallel",)),
        name="rope",
    )(u, *tables)


def _attn_body(lam_ref, g_ref, q_ref, k_ref, v_ref, o_ref, s_ref, rm_ref, m_ref, l_ref, acc_ref,
               *, tk, nkv, post_scale):
    q = q_ref[...]
    lane = lax.broadcasted_iota(jnp.int32, q.shape, 1)
    zero = jnp.zeros_like(q)
    qs = (jnp.where(lane < DA_HEAD_DIM, q, zero), jnp.where(lane >= DA_HEAD_DIM, q, zero))
    nt = tk // LANE
    stat_shape = m_ref.shape[1:]

    def produce(slot, j):
        kk = k_ref[pl.ds(pl.multiple_of(j * tk, tk), tk), :]
        for c in range(2):
            s = lax.dot_general(qs[c], kk, (((1,), (1,)), ((), ())), preferred_element_type=F32)
            s_ref[slot, c] = s
            part = s[:, :LANE]
            for t in range(1, nt):
                part = jnp.maximum(part, s[:, t * LANE:(t + 1) * LANE])
            rm_ref[slot, c] = jnp.broadcast_to(jnp.max(part, axis=-1, keepdims=True), stat_shape)

    def consume(slot, j):
        vv = v_ref[pl.ds(pl.multiple_of(j * tk, tk), tk), :]
        for c in range(2):
            m_prev = m_ref[c]
            m_new = jnp.maximum(m_prev, rm_ref[slot, c])
            alpha = jnp.exp(m_prev - m_new)
            ps = [jnp.exp(s_ref[slot, c, :, t * LANE:(t + 1) * LANE] - m_new) for t in range(nt)]
            l_ref[c] = alpha * l_ref[c] + functools.reduce(lambda a, b: a + b, ps)
            p = jnp.concatenate(ps, axis=1).astype(BF16)
            acc_ref[c] = alpha * acc_ref[c] + jnp.dot(p, vv, preferred_element_type=F32)
            m_ref[c] = m_new

    m_ref[...] = jnp.full(m_ref.shape, -jnp.inf, F32)
    l_ref[...] = jnp.zeros(l_ref.shape, F32)
    acc_ref[...] = jnp.zeros(acc_ref.shape, F32)
    produce(0, 0)
    if nkv == 1:
        consume(0, 0)
    else:
        assert nkv % 2 == 0

        def pair(i, carry):
            j = 2 * i
            produce(1, j + 1)
            consume(0, j)
            produce(0, jnp.where(j + 2 < nkv, j + 2, 0))
            consume(1, j + 1)
            return carry

        lax.fori_loop(0, nkv // 2, pair, 0)
    l0 = jnp.sum(l_ref[0], axis=-1, keepdims=True)
    l1 = jnp.sum(l_ref[1], axis=-1, keepdims=True)
    o = acc_ref[0] / l0 - lam_ref[...] * (acc_ref[1] / l1)
    o_ref[...] = (_rms(o, g_ref[...]) * post_scale).astype(o_ref.dtype)


def _attention(qk, u, row0, nb, seq_len, lam, g_subln, lam_init):
    tq = _tile(seq_len, 1024, 16)
    tk = _tile(seq_len, 1024, LANE)
    nq = seq_len // tq
    assert row0 % seq_len == 0
    rb = row0 // seq_len
    return pl.pallas_call(
        functools.partial(_attn_body, tk=tk, nkv=seq_len // tk, post_scale=1.0 - lam_init),
        grid=(nb, DA_HEADS, nq),
        in_specs=[pl.BlockSpec((1, LANE), lambda b, h, i: (0, 0)),
                  pl.BlockSpec((1, LANE), lambda b, h, i: (0, 0)),
                  pl.BlockSpec((tq, LANE), lambda b, h, i: (b * nq + i, h)),
                  pl.BlockSpec((seq_len, LANE), lambda b, h, i: (b, DA_HEADS + h)),
                  pl.BlockSpec((seq_len, LANE), lambda b, h, i: (rb + b, COL_V // LANE + h))],
        out_specs=pl.BlockSpec((tq, LANE), lambda b, h, i: (b * nq + i, h)),
        out_shape=jax.ShapeDtypeStruct((nb * seq_len, MIX), BF16),
        scratch_shapes=[pltpu.VMEM((2, 2, tq, tk), F32), pltpu.VMEM((2, 2, tq, LANE), F32),
                        pltpu.VMEM((2, tq, LANE), F32), pltpu.VMEM((2, tq, LANE), F32),
                        pltpu.VMEM((2, tq, DA_V_DIM), F32)],
        compiler_params=_cp(("parallel", "parallel", "arbitrary")),
        name="diff_attention",
    )(jnp.full((1, LANE), lam, F32), g_subln.reshape(1, DA_V_DIM).astype(F32), qk, qk, u)


def _dft_cs(n):
    idx = jnp.arange(n, dtype=jnp.int32)
    ang = ((idx[:, None] * idx[None, :]) % n).astype(F32) * (2.0 * math.pi / n)
    return jnp.cos(ang), jnp.sin(ang)


def _fourier_consts(seq_len):
    n1 = 1 << ((seq_len.bit_length() - 1 + 1) // 2)
    n2 = seq_len // n1
    assert n1 * n2 == seq_len and n1 % 16 == 0 and n2 % 16 == 0
    cc, sc = _dft_cs(FN_GROUP_DIM)
    w_chan = jnp.concatenate([cc, -sc], axis=1).astype(BF16)
    c1, s1 = _dft_cs(n1)
    m1 = jnp.concatenate([jnp.concatenate([c1, s1], 1), jnp.concatenate([-s1, c1], 1)], 0).astype(BF16)
    k1 = jnp.arange(n1, dtype=jnp.int32)[:, None]
    i2 = jnp.arange(n2, dtype=jnp.int32)[None, :]
    ang = ((k1 * i2) % seq_len).astype(F32) * (2.0 * math.pi / seq_len)
    expand = lambda t: jnp.repeat(t, FN_GROUP_DIM, axis=1)
    c2, s2 = _dft_cs(n2)
    return dict(n1=n1, n2=n2, w_chan=w_chan, m1=m1, tw_c=expand(jnp.cos(ang)), tw_s=expand(jnp.sin(ang)),
                c2=c2.astype(BF16), s2=s2.astype(BF16))


def _fn_chan_body(x_ref, w_ref, o_ref):
    y = jnp.dot(x_ref[...], w_ref[...], preferred_element_type=F32)
    o_ref[0] = y[:, :FN_GROUP_DIM].astype(o_ref.dtype)
    o_ref[1] = y[:, FN_GROUP_DIM:].astype(o_ref.dtype)


def _fn_stage1_body(y_ref, m1_ref, tc_ref, ts_ref, o_ref, *, n1):
    a = jnp.dot(m1_ref[...], y_ref[...], preferred_element_type=F32)
    ar, ai = a[:n1], a[n1:]
    tc, ts = tc_ref[...], ts_ref[...]
    o_ref[:n1, :] = (ar * tc + ai * ts).astype(o_ref.dtype)
    o_ref[n1:, :] = (ai * tc - ar * ts).astype(o_ref.dtype)


def _fn_stage2_body(a_ref, c2_ref, s2_ref, o_ref, *, kb, norm):
    c2, s2 = c2_ref[...], s2_ref[...]
    for j in range(kb):
        for g in range(FN_GROUPS):
            r = (jnp.dot(c2, a_ref[g, 0, j], preferred_element_type=F32)
                 + jnp.dot(s2, a_ref[g, 1, j], preferred_element_type=F32))
            col = (j * FN_GROUPS + g) * FN_GROUP_DIM
            o_ref[:, col:col + FN_GROUP_DIM] = (r * norm).astype(o_ref.dtype)


def _fourier_mix(u, row0, nb, seq_len, fc):
    n1, n2 = fc["n1"], fc["n2"]
    gd = FN_GROUP_DIM
    tl = _tile(seq_len, 512, 16)
    nl = seq_len // tl
    r0 = row0 // tl
    y = pl.pallas_call(
        _fn_chan_body,
        grid=(nb, FN_GROUPS, nl),
        in_specs=[pl.BlockSpec((tl, gd), lambda b, g, i: (r0 + b * nl + i, COL_FN // gd + g)),
                  pl.BlockSpec((gd, 2 * gd), lambda b, g, i: (0, 0))],
        out_specs=pl.BlockSpec((None, None, 2, tl, gd), lambda b, g, i: (b, g, 0, i, 0)),
        out_shape=jax.ShapeDtypeStruct((nb, FN_GROUPS, 2, seq_len, gd), BF16),
        compiler_params=_cp(("parallel", "parallel", "parallel")),
        name="fnet_channel_dft",
    )(u, fc["w_chan"])

    cols = n2 * gd
    tc = _tile(cols, 2048, LANE)
    a = pl.pallas_call(
        functools.partial(_fn_stage1_body, n1=n1),
        grid=(nb, FN_GROUPS, cols // tc),
        in_specs=[pl.BlockSpec((None, None, 2 * n1, tc), lambda b, g, i: (b, g, 0, i)),
                  pl.BlockSpec((2 * n1, 2 * n1), lambda b, g, i: (0, 0)),
                  pl.BlockSpec((n1, tc), lambda b, g, i: (0, i)),
                  pl.BlockSpec((n1, tc), lambda b, g, i: (0, i))],
        out_specs=pl.BlockSpec((None, None, 2 * n1, tc), lambda b, g, i: (b, g, 0, i)),
        out_shape=jax.ShapeDtypeStruct((nb, FN_GROUPS, 2 * n1, cols), BF16),
        compiler_params=_cp(("parallel", "parallel", "parallel")),
        name="fnet_seq_dft_stage1",
    )(y.reshape(nb, FN_GROUPS, 2 * n1, cols), fc["m1"], fc["tw_c"], fc["tw_s"])

    kb = _tile(n1, 8, 1)
    out = pl.pallas_call(
        functools.partial(_fn_stage2_body, kb=kb, norm=1.0 / math.sqrt(seq_len * gd)),
        grid=(nb, n1 // kb),
        in_specs=[pl.BlockSpec((None, FN_GROUPS, 2, kb, n2, gd), lambda b, i: (b, 0, 0, i, 0, 0)),
                  pl.BlockSpec((n2, n2), lambda b, i: (0, 0)),
                  pl.BlockSpec((n2, n2), lambda b, i: (0, 0))],
        out_specs=pl.BlockSpec((None, n2, kb * MIX), lambda b, i: (b, 0, i)),
        out_shape=jax.ShapeDtypeStruct((nb, n2, n1 * MIX), BF16),
        compiler_params=_cp(("parallel", "parallel")),
        name="fnet_seq_dft_stage2",
    )(a.reshape(nb, FN_GROUPS, 2, n1, n2, gd), fc["c2"], fc["s2"])
    return out.reshape(nb * seq_len, MIX)


def _conv_body(ap_ref, ac_ref, an_ref, gp_ref, gc_ref, gn_ref, w_ref, b_ref, lg_ref, lb_ref, o_ref, z_ref,
               *, tl, nl):
    i = pl.program_id(1)
    glu = lambda a_ref, g_ref: a_ref[...].astype(F32) * jax.nn.sigmoid(g_ref[...].astype(F32))
    z_ref[0:CONV_HALO, :] = jnp.where(i > 0, glu(ap_ref, gp_ref), 0.0)
    z_ref[CONV_HALO:CONV_HALO + tl, :] = glu(ac_ref, gc_ref)
    z_ref[CONV_HALO + tl:2 * CONV_HALO + tl, :] = jnp.where(i < nl - 1, glu(an_ref, gn_ref), 0.0)
    acc = jnp.broadcast_to(b_ref[...], (tl, MIX))
    first = CONV_HALO - CONV_TAPS // 2
    for j in range(CONV_TAPS):
        acc = acc + w_ref[j:j + 1, :] * z_ref[first + j:first + j + tl, :]
    mu = jnp.mean(acc, axis=-1, keepdims=True)
    xc = acc - mu
    var = jnp.mean(xc * xc, axis=-1, keepdims=True)
    y = xc * lax.rsqrt(var + LN_EPS) * lg_ref[...] + lb_ref[...]
    o_ref[...] = jax.nn.silu(y).astype(o_ref.dtype)


def _conformer_conv(u, row0, nb, seq_len, w_dw, b_dw, ln_g, ln_b):
    m = u.shape[0]
    tl = _tile(seq_len, 256, CONV_HALO)
    nl = seq_len // tl
    r0 = row0 // tl
    hb = tl // CONV_HALO
    last_hb = m // CONV_HALO - 1
    ca, cg = COL_CV // MIX, COL_CV // MIX + 1

    def cur(c):
        return pl.BlockSpec((tl, MIX), lambda b, i: (r0 + b * nl + i, c))

    def prev(c):
        return pl.BlockSpec((CONV_HALO, MIX), lambda b, i: (jnp.maximum((r0 + b * nl + i) * hb - 1, 0), c))

    def nxt(c):
        return pl.BlockSpec((CONV_HALO, MIX), lambda b, i: (jnp.minimum((r0 + b * nl + i + 1) * hb, last_hb), c))

    vec = pl.BlockSpec((1, MIX), lambda b, i: (0, 0))
    return pl.pallas_call(
        functools.partial(_conv_body, tl=tl, nl=nl),
        grid=(nb, nl),
        in_specs=[prev(ca), cur(ca), nxt(ca), prev(cg), cur(cg), nxt(cg),
                  pl.BlockSpec((CONV_TAPS, MIX), lambda b, i: (0, 0)), vec, vec, vec],
        out_specs=pl.BlockSpec((tl, MIX), lambda b, i: (b * nl + i, 0)),
        out_shape=jax.ShapeDtypeStruct((nb * seq_len, MIX), BF16),
        scratch_shapes=[pltpu.VMEM((tl + 2 * CONV_HALO, MIX), F32)],
        compiler_params=_cp(("parallel", "arbitrary")),
        name="conformer_conv",
    )(u, u, u, u, u, u, w_dw, b_dw.reshape(1, MIX), ln_g.reshape(1, MIX), ln_b.reshape(1, MIX))


def _log_sigmoid(z):
    return jnp.minimum(z, 0.0) - jnp.log1p(jnp.exp(-jnp.abs(z)))


def _gla_scan_body(qf_ref, kf_ref, vf_ref, gaf_ref, qb_ref, kb_ref, vb_ref, gab_ref, w2_ref, b2_ref,
                   of_ref, ob_ref, st_ref):
    @pl.when(pl.program_id(1) == 0)
    def _():
        st_ref[...] = jnp.zeros(st_ref.shape, F32)

    c = GLA_CHUNK
    row = lax.broadcasted_iota(jnp.int32, (c, c), 0)
    col = lax.broadcasted_iota(jnp.int32, (c, c), 1)
    kw = GLA_HEADS * GLA_DK
    dirs = ((qf_ref, kf_ref, vf_ref, gaf_ref, of_ref, col <= row),
            (qb_ref, kb_ref, vb_ref, gab_ref, ob_ref, col >= row))
    for d, (q_ref, k_ref, v_ref, ga_ref, o_ref, keep) in enumerate(dirs):
        z = jnp.dot(ga_ref[...], w2_ref[:, d * kw:(d + 1) * kw], preferred_element_type=F32)
        g_all = _log_sigmoid(z + b2_ref[:, d * kw:(d + 1) * kw]) / GLA_TAU
        tri = keep.astype(F32)
        for h in range(GLA_HEADS):
            ks = slice(h * GLA_DK, (h + 1) * GLA_DK)
            vs = slice(h * GLA_DV, (h + 1) * GLA_DV)
            g = g_all[:, ks]
            b = jnp.dot(tri, g, preferred_element_type=F32, precision=lax.Precision.HIGHEST)
            b_tot = jnp.sum(g, axis=0, keepdims=True)
            q = q_ref[:, ks].astype(F32) * (GLA_DK ** -0.5)
            k = k_ref[:, ks].astype(F32)
            v = v_ref[:, vs]
            q_dec = (q * jnp.exp(b)).astype(BF16)
            k_inv = (k * jnp.exp(-b)).astype(BF16)
            k_end = (k * jnp.exp(b_tot - b)).astype(BF16)
            st = st_ref[d, h]
            o_inter = lax.dot_general(q_dec, st.astype(BF16), (((1,), (1,)), ((), ())),
                                      preferred_element_type=F32)
            a = lax.dot_general(q_dec, k_inv, (((1,), (1,)), ((), ())), preferred_element_type=F32)
            a = jnp.where(keep, a, 0.0)
            o_intra = jnp.dot(a.astype(BF16), v, preferred_element_type=F32)
            o_ref[:, vs] = o_inter + o_intra
            st_ref[d, h] = st * jnp.exp(b_tot) + lax.dot_general(
                v, k_end, (((0,), (0,)), ((), ())), preferred_element_type=F32)


def _gla_post_body(of_ref, ob_ref, r_ref, g_ref, o_ref):
    g = g_ref[...]
    for h in range(GLA_HEADS):
        vs = slice(h * GLA_DV, (h + 1) * GLA_DV)
        o = _rms(of_ref[:, vs] + ob_ref[:, vs], g)
        o_ref[:, vs] = (o * jax.nn.silu(r_ref[:, vs].astype(F32))).astype(o_ref.dtype)


def _gla(u, ga, row0, nb, seq_len, w2, b2, g_norm):
    c = GLA_CHUNK
    n = seq_len // c
    r0 = row0 // c
    kw = GLA_HEADS * GLA_DK
    fwd = lambda col: (lambda b, i: (r0 + b * n + i, col))
    bwd = lambda col: (lambda b, i: (r0 + b * n + n - 1 - i, col))
    specs = []
    for imap in (fwd, bwd):
        specs += [pl.BlockSpec((c, kw), imap(COL_GQ // kw)), pl.BlockSpec((c, kw), imap(COL_GK // kw)),
                  pl.BlockSpec((c, MIX), imap(COL_GV // MIX)), pl.BlockSpec((c, LANE), imap(0))]
    specs += [pl.BlockSpec((LANE, 2 * kw), lambda b, i: (0, 0)), pl.BlockSpec((1, 2 * kw), lambda b, i: (0, 0))]
    o_shape = jax.ShapeDtypeStruct((nb * seq_len, MIX), F32)
    o_f, o_b = pl.pallas_call(
        _gla_scan_body,
        grid=(nb, n),
        in_specs=specs,
        out_specs=[pl.BlockSpec((c, MIX), lambda b, i: (b * n + i, 0)),
                   pl.BlockSpec((c, MIX), lambda b, i: (b * n + n - 1 - i, 0))],
        out_shape=[o_shape, o_shape],
        scratch_shapes=[pltpu.VMEM((2, GLA_HEADS, GLA_DV, GLA_DK), F32)],
        compiler_params=_cp(("parallel", "arbitrary")),
        name="gla_scan",
    )(u, u, u, ga, u, u, u, ga, w2, b2)

    tl = _tile(seq_len, 256, 16)
    nl = seq_len // tl
    rr = row0 // tl
    row_spec = pl.BlockSpec((tl, MIX), lambda i: (i, 0))
    return pl.pallas_call(
        _gla_post_body,
        grid=(nb * nl,),
        in_specs=[row_spec, row_spec, pl.BlockSpec((tl, MIX), lambda i: (rr + i, COL_GR // MIX)),
                  pl.BlockSpec((1, GLA_DV), lambda i: (0, 0))],
        out_specs=row_spec,
        out_shape=jax.ShapeDtypeStruct((nb * seq_len, MIX), BF16),
        compiler_params=_cp(("parallel",)),
        name="gla_post",
    )(o_f, o_b, u, g_norm.reshape(1, GLA_DV))


def _merge_body(h_ref, wg_ref, bg_ref, z_ref, wb_ref, o_ref, acc_ref, *, n_branch):
    i = pl.program_id(2)
    gate = jax.nn.sigmoid(jnp.dot(h_ref[...], wg_ref[...], preferred_element_type=F32) + bg_ref[...])
    contrib = gate * jnp.dot(z_ref[...], wb_ref[...], preferred_element_type=F32)

    @pl.when(i == 0)
    def _():
        acc_ref[...] = contrib

    @pl.when(i > 0)
    def _():
        acc_ref[...] += contrib

    @pl.when(i == n_branch - 1)
    def _():
        o_ref[...] = acc_ref[...].astype(o_ref.dtype)


def _merge(h, z, w_gate, b_gate, w_branch, layer, tm=1024, tn=512):
    m, d = h.shape
    n_branch = z.shape[0]
    tm, tn = _tile(m, tm), _tile(d, tn, LANE)
    return pl.pallas_call(
        functools.partial(_merge_body, n_branch=n_branch),
        grid=(m // tm, d // tn, n_branch),
        in_specs=[pl.BlockSpec((tm, d), lambda a, j, i: (a, 0)),
                  pl.BlockSpec((None, None, d, tn), lambda a, j, i: (layer, i, 0, j)),
                  pl.BlockSpec((None, None, 1, tn), lambda a, j, i: (layer, i, 0, j)),
                  pl.BlockSpec((None, tm, MIX), lambda a, j, i: (i, a, 0)),
                  pl.BlockSpec((None, None, MIX, tn), lambda a, j, i: (layer, i, 0, j))],
        out_specs=pl.BlockSpec((tm, tn), lambda a, j, i: (a, j)),
        out_shape=jax.ShapeDtypeStruct((m, d), BF16),
        scratch_shapes=[pltpu.VMEM((tm, tn), F32)],
        compiler_params=_cp(("parallel", "parallel", "arbitrary")),
        name="gated_merge",
    )(h, w_gate, b_gate.reshape(b_gate.shape[0], n_branch, 1, d), z, w_branch)


def _router_body(x_ref, g_ref, wr_ref, h_ref, gate_ref, *, n_experts):
    hb = _rms(x_ref[...], g_ref[...]).astype(BF16)
    h_ref[...] = hb
    logits = jnp.dot(hb, wr_ref[...], preferred_element_type=F32)
    lane = lax.broadcasted_iota(jnp.int32, logits.shape, 1).astype(F32)
    neg = jnp.float32(-jnp.inf)
    big = jnp.float32(LANE)
    lg = jnp.where(lane < n_experts, logits, neg)
    m1 = jnp.max(lg, axis=-1, keepdims=True)
    i1 = jnp.min(jnp.where(lg == m1, lane, big), axis=-1, keepdims=True)
    lg2 = jnp.where(lane == i1, neg, lg)
    m2 = jnp.max(lg2, axis=-1, keepdims=True)
    i2 = jnp.min(jnp.where(lg2 == m2, lane, big), axis=-1, keepdims=True)
    e2 = jnp.exp(m2 - m1)
    denom = 1.0 + e2
    gate_ref[...] = jnp.where(lane == i1, 1.0 / denom, 0.0) + jnp.where(lane == i2, e2 / denom, 0.0)


def _norm_router(x, g, w_router_pad, n_experts):
    m, d = x.shape
    tm = _tile(m, 256)
    return pl.pallas_call(
        functools.partial(_router_body, n_experts=n_experts),
        grid=(m // tm,),
        in_specs=[pl.BlockSpec((tm, d), lambda i: (i, 0)),
                  pl.BlockSpec((1, d), lambda i: (0, 0)),
                  pl.BlockSpec((d, LANE), lambda i: (0, 0))],
        out_specs=[pl.BlockSpec((tm, d), lambda i: (i, 0)), pl.BlockSpec((tm, LANE), lambda i: (i, 0))],
        out_shape=[jax.ShapeDtypeStruct((m, d), BF16), jax.ShapeDtypeStruct((m, LANE), F32)],
        compiler_params=_cp(("parallel",)),
        name="norm_router",
    )(x, g.reshape(1, d), w_router_pad)


def kernel(x_prompt, x_sample, norm_mix, w_in, da_lambda, da_subln_g, cv_dw_w, cv_dw_b, cv_ln_g, cv_ln_b,
           gla_w_a2, gla_b_a2, gla_norm_g, w_gate, b_gate, w_branch, w_out, norm_ffn, ffn_w_gate, ffn_w_up,
           ffn_w_down, moe_router, moe_w_gate, moe_w_up, moe_w_down, norm_final):
    depth = norm_mix.shape[0]
    d = x_prompt.shape[-1]
    seqs = []
    row = 0
    for xs in (x_prompt, x_sample):
        seqs.append((row, xs.shape[0], xs.shape[1]))
        row += xs.shape[0] * xs.shape[1]
    x = jnp.concatenate([x_prompt.reshape(-1, d), x_sample.reshape(-1, d)], axis=0)

    w_in_b = w_in.astype(BF16)
    w_ga_b = jnp.pad(w_in[:, :, COL_GA:], ((0, 0), (0, 0), (0, LANE - 2 * GLA_RANK))).astype(BF16)
    w_gate_b = w_gate.astype(BF16)
    w_branch_b = w_branch.astype(BF16)
    w_out_b = w_out.astype(BF16)
    d_ff = ffn_w_gate.shape[-1]
    f_pad = _round_up(d_ff, 1024 if d_ff >= 1024 else LANE) - d_ff
    ffn_wg_b = jnp.pad(ffn_w_gate, ((0, 0), (0, 0), (0, f_pad))).astype(BF16)
    ffn_wu_b = jnp.pad(ffn_w_up, ((0, 0), (0, 0), (0, f_pad))).astype(BF16)
    ffn_wd_b = jnp.pad(ffn_w_down, ((0, 0), (0, f_pad), (0, 0))).astype(BF16)
    n_experts, f_exp = moe_w_gate.shape[1], moe_w_gate.shape[-1]
    moe_wg_b = moe_w_gate.astype(BF16)
    moe_wu_b = moe_w_up.astype(BF16)
    moe_wd_b = moe_w_down.astype(BF16).reshape(moe_w_down.shape[0], n_experts * f_exp, d)
    router_b = jnp.pad(moe_router, ((0, 0), (0, 0), (0, LANE - n_experts))).astype(BF16)
    kw = GLA_HEADS * GLA_DK
    w2 = jnp.zeros((depth, LANE, 2 * kw), F32)
    w2 = w2.at[:, :GLA_RANK, :kw].set(gla_w_a2[:, 0]).at[:, GLA_RANK:2 * GLA_RANK, kw:].set(gla_w_a2[:, 1])
    w2 = w2.astype(BF16)
    b2 = gla_b_a2.reshape(depth, 1, 2 * kw)

    rope_tabs = {s[2]: _rope_tables(s[2]) for s in seqs}
    fn_consts = {s[2]: _fourier_consts(s[2]) for s in seqs}

    for l in range(depth):
        h = _rmsnorm(x, norm_mix[l], BF16)
        u = _mm(h, w_in_b, (l,), COL_GA, BF16, name="in_proj")
        ga = _mm(h, w_ga_b, (l,), LANE, BF16, name="gla_gate_proj")
        lam_init = 0.8 - 0.6 * math.exp(-0.3 * l)
        lp = da_lambda[l].astype(F32)
        lam = jnp.exp(jnp.sum(lp[0] * lp[1])) - jnp.exp(jnp.sum(lp[2] * lp[3])) + lam_init
        zs = [[], [], [], []]
        for row0, nb, seq_len in seqs:
            qk = _rope(u, row0, nb, seq_len, rope_tabs[seq_len])
            zs[0].append(_attention(qk, u, row0, nb, seq_len, lam, da_subln_g[l], lam_init))
            zs[1].append(_fourier_mix(u, row0, nb, seq_len, fn_consts[seq_len]))
            zs[2].append(_conformer_conv(u, row0, nb, seq_len, cv_dw_w[l], cv_dw_b[l], cv_ln_g[l], cv_ln_b[l]))
            zs[3].append(_gla(u, ga, row0, nb, seq_len, w2[l], b2[l], gla_norm_g[l]))
        z = jnp.stack([jnp.concatenate(parts, axis=0) for parts in zs], axis=0)
        merged = _merge(h, z, w_gate_b, b_gate, w_branch_b, l)
        x = _mm(merged, w_out_b, (l,), d, F32, res=x, tn=512, name="out_proj")
        j = l // 2
        if l % 2 == 0:
            h2 = _rmsnorm(x, norm_ffn[l], BF16)
            mid = _glu_up(h2, ffn_wg_b, ffn_wu_b, (j,), 1, d_ff + f_pad)
            x = _mm(mid, ffn_wd_b, (j,), d, F32, res=x, tk=(d_ff + f_pad) // 4, name="ffn_down")
        else:
            h2, gates = _norm_router(x, norm_ffn[l], router_b[j], n_experts)
            mid = _glu_up(h2, moe_wg_b, moe_wu_b, (j,), n_experts, f_exp, gates=gates)
            x = _mm(mid, moe_wd_b, (j,), d, F32, res=x, tk=n_experts * f_exp // 4, name="moe_down")
    outs = [_rmsnorm(x, norm_final, F32, row0, nb * seq_len) for row0, nb, seq_len in seqs]
    return outs[0].reshape(x_prompt.shape), outs[1].reshape(x_sample.shape)
```

```python
import functools
import math

import jax
import jax.numpy as jnp
from jax import lax
from jax.experimental import pallas as pl
from jax.experimental.pallas import tpu as pltpu

F32 = jnp.float32
BF16 = jnp.bfloat16

MIX = 1024
DA_HEADS = 8
DA_HEAD_DIM = 64
DA_V_DIM = 128
ROPE_DIM = 16
ROPE_THETA = 500000.0
FN_GROUPS = 4
FN_GROUP_DIM = 256
CONV_TAPS = 31
CONV_HALO = 16
GLA_HEADS = 4
GLA_DK = 128
GLA_DV = 256
GLA_RANK = 16
GLA_TAU = 16.0
GLA_CHUNK = 64
GLA_CHUNKS_PER_STEP = 4
TOP_K = 2
RMS_EPS = 1e-6
LN_EPS = 1e-5

COL_Q = 0
COL_K = 1024
COL_V = 2048
COL_FN = 3072
COL_CV = 4096
COL_GQ = 6144
COL_GK = 6656
COL_GV = 7168
COL_GR = 8192
COL_GA = 9216
LANE = 128
SUBLANES = 8

VMEM_LIMIT_MB = 56


def _cp(sem, vmem_mb=VMEM_LIMIT_MB):
    return pltpu.CompilerParams(dimension_semantics=sem, vmem_limit_bytes=vmem_mb * 2**20)


def _tile(n, pref, align=8):
    if n <= pref:
        return n
    t = (pref // align) * align
    while t >= align:
        if n % t == 0:
            return t
        t -= align
    raise ValueError(f"no tile for {n} (pref {pref}, align {align})")


def _round_up(n, m):
    return (n + m - 1) // m * m


def _rms(x, g):
    ms = jnp.mean(x * x, axis=-1, keepdims=True)
    return x * lax.rsqrt(ms + RMS_EPS) * g


def _rmsnorm_body(x_ref, g_ref, o_ref):
    o_ref[...] = _rms(x_ref[...], g_ref[...]).astype(o_ref.dtype)


def _rmsnorm(x, g, out_dtype, row0=0, rows=None):
    d = x.shape[1]
    m = x.shape[0] if rows is None else rows
    tm = math.gcd(_tile(m, 256), row0) if row0 else _tile(m, 256)
    r0 = row0 // tm
    return pl.pallas_call(
        _rmsnorm_body,
        grid=(m // tm,),
        in_specs=[pl.BlockSpec((tm, d), lambda i: (r0 + i, 0)),
                  pl.BlockSpec((1, d), lambda i: (0, 0))],
        out_specs=pl.BlockSpec((tm, d), lambda i: (i, 0)),
        out_shape=jax.ShapeDtypeStruct((m, d), out_dtype),
        compiler_params=_cp(("parallel",)),
        name="rmsnorm",
    )(x, g.reshape(1, d))


def _mm_body(*refs, nk, has_res):
    if has_res:
        a_ref, w_ref, r_ref, o_ref = refs
    else:
        a_ref, w_ref, o_ref = refs
        r_ref = None

    def first():
        r = jnp.dot(a_ref[...], w_ref[...], preferred_element_type=F32)
        if has_res:
            r = r + r_ref[...]
        o_ref[...] = r.astype(o_ref.dtype)

    if nk == 1:
        first()
        return
    k = pl.program_id(2)
    pl.when(k == 0)(first)

    @pl.when(k > 0)
    def _():
        o_ref[...] += jnp.dot(a_ref[...], w_ref[...], preferred_element_type=F32)


def _mm(a, w, lead, n, out_dtype, res=None, tm=1024, tn=1024, tk=None, name="mm"):
    m, kdim = a.shape
    tm, tn = _tile(m, tm), _tile(n, tn, LANE)
    tk = kdim if tk is None else _tile(kdim, tk, LANE)
    nk = kdim // tk
    assert nk == 1 or out_dtype == F32
    nlead = len(lead)
    in_specs = [pl.BlockSpec((tm, tk), lambda i, j, k: (i, k)),
                pl.BlockSpec((None,) * nlead + (tk, tn), lambda i, j, k: lead + (k, j))]
    args = [a, w]
    if res is not None:
        in_specs.append(pl.BlockSpec((tm, tn), lambda i, j, k: (i, j)))
        args.append(res)
    return pl.pallas_call(
        functools.partial(_mm_body, nk=nk, has_res=res is not None),
        grid=(m // tm, n // tn, nk),
        in_specs=in_specs,
        out_specs=pl.BlockSpec((tm, tn), lambda i, j, k: (i, j)),
        out_shape=jax.ShapeDtypeStruct((m, n), out_dtype),
        compiler_params=_cp(("parallel", "parallel", "arbitrary")),
        name=name,
    )(*args)


def _glu_body(*refs, has_gate, blocks_per_expert):
    if has_gate:
        a_ref, wg_ref, wu_ref, gate_ref, o_ref = refs
    else:
        a_ref, wg_ref, wu_ref, o_ref = refs
        gate_ref = None
    a = a_ref[...]
    hg = jnp.dot(a, wg_ref[...], preferred_element_type=F32)
    hu = jnp.dot(a, wu_ref[...], preferred_element_type=F32)
    h = jax.nn.silu(hg) * hu
    if has_gate:
        e = (pl.program_id(1) // blocks_per_expert).astype(F32)
        gates = gate_ref[...]
        lane = lax.broadcasted_iota(jnp.int32, gates.shape, 1).astype(F32)
        h = h * jnp.sum(jnp.where(lane == e, gates, 0.0), axis=-1, keepdims=True)
    o_ref[...] = h.astype(o_ref.dtype)


def _glu_up(a, wg, wu, lead, n_experts, f, gates=None, tm=1024, tn=512):
    m, kdim = a.shape
    tm = _tile(m, tm)
    tn = tn if f % tn == 0 else _tile(f, tn // 2, LANE)
    bpe = f // tn
    nlead = len(lead)
    if gates is not None:
        w_spec = pl.BlockSpec((None,) * (nlead + 1) + (kdim, tn), lambda i, j: lead + (j // bpe, 0, j % bpe))
    else:
        w_spec = pl.BlockSpec((None,) * nlead + (kdim, tn), lambda i, j: lead + (0, j))
    in_specs = [pl.BlockSpec((tm, kdim), lambda i, j: (i, 0)), w_spec, w_spec]
    args = [a, wg, wu]
    if gates is not None:
        in_specs.append(pl.BlockSpec((tm, LANE), lambda i, j: (i, 0)))
        args.append(gates)
    return pl.pallas_call(
        functools.partial(_glu_body, has_gate=gates is not None, blocks_per_expert=bpe),
        grid=(m // tm, n_experts * bpe),
        in_specs=in_specs,
        out_specs=pl.BlockSpec((tm, tn), lambda i, j: (i, j)),
        out_shape=jax.ShapeDtypeStruct((m, n_experts * f), BF16),
        compiler_params=_cp(("parallel", "arbitrary")),
        name="glu_up",
    )(*args)


def _rope_tables(seq_len):
    half = ROPE_DIM // 2
    inv_freq = 1.0 / (ROPE_THETA ** (jnp.arange(half, dtype=F32) * 2.0 / ROPE_DIM))
    ang = jnp.arange(seq_len, dtype=F32)[:, None] * inv_freq[None, :]
    cos, sin = jnp.cos(ang), jnp.sin(ang)
    one = jnp.ones((seq_len, DA_HEAD_DIM - ROPE_DIM), F32)
    zero = jnp.zeros((seq_len, DA_HEAD_DIM - ROPE_DIM), F32)
    zh = jnp.zeros_like(sin)
    comp = lambda a, b, rest: jnp.concatenate([a, b, rest], axis=-1)
    c = comp(cos, cos, one)
    s_from_lower = comp(zh, sin, zero)
    s_from_upper = comp(-sin, zh, zero)
    two = lambda t: jnp.concatenate([t, t], axis=-1)
    return two(c), two(s_from_lower), two(s_from_upper)


def _rope_body(x_ref, c_ref, sl_ref, su_ref, o_ref, *, q_scale):
    c, sl, su = c_ref[...], sl_ref[...], su_ref[...]
    n_blocks = 2 * DA_HEADS
    for j in range(n_blocks):
        t = x_ref[:, j * LANE:(j + 1) * LANE].astype(F32)
        r = t * c + pltpu.roll(t, ROPE_DIM // 2, 1) * sl + pltpu.roll(t, LANE - ROPE_DIM // 2, 1) * su
        if j < DA_HEADS:
            r = r * q_scale
        o_ref[:, j * LANE:(j + 1) * LANE] = r.astype(o_ref.dtype)


def _rope(u, row0, nb, seq_len, tables):
    tl = _tile(seq_len, 512, 16)
    nl = seq_len // tl
    r0 = row0 // tl
    width = 2 * MIX
    tab_spec = pl.BlockSpec((tl, LANE), lambda i: (i % nl, 0))
    return pl.pallas_call(
        functools.partial(_rope_body, q_scale=DA_HEAD_DIM ** -0.5),
        grid=(nb * nl,),
        in_specs=[pl.BlockSpec((tl, width), lambda i: (r0 + i, 0)), tab_spec, tab_spec, tab_spec],
        out_specs=pl.BlockSpec((tl, width), lambda i: (i, 0)),
        out_shape=jax.ShapeDtypeStruct((nb * seq_len, width), BF16),
        compiler_params=_cp(("parallel",)),
        name="rope",
    )(u, *tables)


def _attn_body(lam_ref, g_ref, q_ref, k_ref, v_ref, o_ref, s_ref, rm_ref, m_ref, l_ref, acc_ref,
               *, tk, nkv, post_scale):
    q = q_ref[...]
    lane = lax.broadcasted_iota(jnp.int32, q.shape, 1)
    zero = jnp.zeros_like(q)
    qs = (jnp.where(lane < DA_HEAD_DIM, q, zero), jnp.where(lane >= DA_HEAD_DIM, q, zero))
    nt = tk // LANE
    stat_shape = m_ref.shape[1:]

    def produce(slot, j):
        kk = k_ref[pl.ds(pl.multiple_of(j * tk, tk), tk), :]
        for c in range(2):
            s = lax.dot_general(qs[c], kk, (((1,), (1,)), ((), ())), preferred_element_type=F32)
            s_ref[slot, c] = s
            part = s[:, :LANE]
            for t in range(1, nt):
                part = jnp.maximum(part, s[:, t * LANE:(t + 1) * LANE])
            rm_ref[slot, c] = jnp.broadcast_to(jnp.max(part, axis=-1, keepdims=True), stat_shape)

    def consume(slot, j):
        vv = v_ref[pl.ds(pl.multiple_of(j * tk, tk), tk), :]
        for c in range(2):
            m_prev = m_ref[c]
            m_new = jnp.maximum(m_prev, rm_ref[slot, c])
            alpha = jnp.exp(m_prev - m_new)
            ps = [jnp.exp(s_ref[slot, c, :, t * LANE:(t + 1) * LANE] - m_new) for t in range(nt)]
            l_ref[c] = alpha * l_ref[c] + functools.reduce(lambda a, b: a + b, ps)
            p = jnp.concatenate(ps, axis=1).astype(BF16)
            acc_ref[c] = alpha * acc_ref[c] + jnp.dot(p, vv, preferred_element_type=F32)
            m_ref[c] = m_new

    m_ref[...] = jnp.full(m_ref.shape, -jnp.inf, F32)
    l_ref[...] = jnp.zeros(l_ref.shape, F32)
    acc_ref[...] = jnp.zeros(acc_ref.shape, F32)
    produce(0, 0)
    if nkv == 1:
        consume(0, 0)
    else:
        assert nkv % 2 == 0

        def pair(i, carry):
            j = 2 * i
            produce(1, j + 1)
            consume(0, j)
            produce(0, jnp.where(j + 2 < nkv, j + 2, 0))
            consume(1, j + 1)
            return carry

        lax.fori_loop(0, nkv // 2, pair, 0)
    l0 = jnp.sum(l_ref[0], axis=-1, keepdims=True)
    l1 = jnp.sum(l_ref[1], axis=-1, keepdims=True)
    o = acc_ref[0] / l0 - lam_ref[...] * (acc_ref[1] / l1)
    o_ref[...] = (_rms(o, g_ref[...]) * post_scale).astype(o_ref.dtype)


def _attention(qk, u, row0, nb, seq_len, lam, g_subln, lam_init):
    tq = _tile(seq_len, 1024, 16)
    tk = _tile(seq_len, 1024, LANE)
    nq = seq_len // tq
    assert row0 % seq_len == 0
    rb = row0 // seq_len
    return pl.pallas_call(
        functools.partial(_attn_body, tk=tk, nkv=seq_len // tk, post_scale=1.0 - lam_init),
        grid=(nb, DA_HEADS, nq),
        in_specs=[pl.BlockSpec((1, LANE), lambda b, h, i: (0, 0)),
                  pl.BlockSpec((1, LANE), lambda b, h, i: (0, 0)),
                  pl.BlockSpec((tq, LANE), lambda b, h, i: (b * nq + i, h)),
                  pl.BlockSpec((seq_len, LANE), lambda b, h, i: (b, DA_HEADS + h)),
                  pl.BlockSpec((seq_len, LANE), lambda b, h, i: (rb + b, COL_V // LANE + h))],
        out_specs=pl.BlockSpec((tq, LANE), lambda b, h, i: (b * nq + i, h)),
        out_shape=jax.ShapeDtypeStruct((nb * seq_len, MIX), BF16),
        scratch_shapes=[pltpu.VMEM((2, 2, tq, tk), F32), pltpu.VMEM((2, 2, tq, LANE), F32),
                        pltpu.VMEM((2, tq, LANE), F32), pltpu.VMEM((2, tq, LANE), F32),
                        pltpu.VMEM((2, tq, DA_V_DIM), F32)],
        compiler_params=_cp(("parallel", "parallel", "arbitrary")),
        name="diff_attention",
    )(jnp.full((1, LANE), lam, F32), g_subln.reshape(1, DA_V_DIM).astype(F32), qk, qk, u)


def _dft_cs(n):
    idx = jnp.arange(n, dtype=jnp.int32)
    ang = ((idx[:, None] * idx[None, :]) % n).astype(F32) * (2.0 * math.pi / n)
    return jnp.cos(ang), jnp.sin(ang)


def _fourier_consts(seq_len):
    n1 = 1 << ((seq_len.bit_length() - 1 + 1) // 2)
    n2 = seq_len // n1
    assert n1 * n2 == seq_len and n1 % 16 == 0 and n2 % 16 == 0
    cc, sc = _dft_cs(FN_GROUP_DIM)
    w_chan = jnp.concatenate([cc, -sc], axis=1).astype(BF16)
    c1, s1 = _dft_cs(n1)
    m1 = jnp.concatenate([jnp.concatenate([c1, s1], 1), jnp.concatenate([-s1, c1], 1)], 0).astype(BF16)
    k1 = jnp.arange(n1, dtype=jnp.int32)[:, None]
    i2 = jnp.arange(n2, dtype=jnp.int32)[None, :]
    ang = ((k1 * i2) % seq_len).astype(F32) * (2.0 * math.pi / seq_len)
    expand = lambda t: jnp.repeat(t, FN_GROUP_DIM, axis=1)
    c2, s2 = _dft_cs(n2)
    return dict(n1=n1, n2=n2, w_chan=w_chan, m1=m1, tw_c=expand(jnp.cos(ang)), tw_s=expand(jnp.sin(ang)),
                c2=c2.astype(BF16), s2=s2.astype(BF16))


def _fn_chan_body(x_ref, w_ref, o_ref):
    y = jnp.dot(x_ref[...], w_ref[...], preferred_element_type=F32)
    o_ref[0] = y[:, :FN_GROUP_DIM].astype(o_ref.dtype)
    o_ref[1] = y[:, FN_GROUP_DIM:].astype(o_ref.dtype)


def _fn_stage1_body(y_ref, m1_ref, tc_ref, ts_ref, o_ref, *, n1):
    a = jnp.dot(m1_ref[...], y_ref[...], preferred_element_type=F32)
    ar, ai = a[:n1], a[n1:]
    tc, ts = tc_ref[...], ts_ref[...]
    o_ref[:n1, :] = (ar * tc + ai * ts).astype(o_ref.dtype)
    o_ref[n1:, :] = (ai * tc - ar * ts).astype(o_ref.dtype)


def _fn_stage2_body(a_ref, c2_ref, s2_ref, o_ref, *, kb, norm):
    c2, s2 = c2_ref[...], s2_ref[...]
    for j in range(kb):
        for g in range(FN_GROUPS):
            r = (jnp.dot(c2, a_ref[g, 0, j], preferred_element_type=F32)
                 + jnp.dot(s2, a_ref[g, 1, j], preferred_element_type=F32))
            col = (j * FN_GROUPS + g) * FN_GROUP_DIM
            o_ref[:, col:col + FN_GROUP_DIM] = (r * norm).astype(o_ref.dtype)


def _fourier_mix(u, row0, nb, seq_len, fc):
    n1, n2 = fc["n1"], fc["n2"]
    gd = FN_GROUP_DIM
    tl = _tile(seq_len, 512, 16)
    nl = seq_len // tl
    r0 = row0 // tl
    y = pl.pallas_call(
        _fn_chan_body,
        grid=(nb, FN_GROUPS, nl),
        in_specs=[pl.BlockSpec((tl, gd), lambda b, g, i: (r0 + b * nl + i, COL_FN // gd + g)),
                  pl.BlockSpec((gd, 2 * gd), lambda b, g, i: (0, 0))],
        out_specs=pl.BlockSpec((None, None, 2, tl, gd), lambda b, g, i: (b, g, 0, i, 0)),
        out_shape=jax.ShapeDtypeStruct((nb, FN_GROUPS, 2, seq_len, gd), BF16),
        compiler_params=_cp(("parallel", "parallel", "parallel")),
        name="fnet_channel_dft",
    )(u, fc["w_chan"])

    cols = n2 * gd
    tc = _tile(cols, 2048, LANE)
    a = pl.pallas_call(
        functools.partial(_fn_stage1_body, n1=n1),
        grid=(nb, FN_GROUPS, cols // tc),
        in_specs=[pl.BlockSpec((None, None, 2 * n1, tc), lambda b, g, i: (b, g, 0, i)),
                  pl.BlockSpec((2 * n1, 2 * n1), lambda b, g, i: (0, 0)),
                  pl.BlockSpec((n1, tc), lambda b, g, i: (0, i)),
                  pl.BlockSpec((n1, tc), lambda b, g, i: (0, i))],
        out_specs=pl.BlockSpec((None, None, 2 * n1, tc), lambda b, g, i: (b, g, 0, i)),
        out_shape=jax.ShapeDtypeStruct((nb, FN_GROUPS, 2 * n1, cols), BF16),
        compiler_params=_cp(("parallel", "parallel", "parallel")),
        name="fnet_seq_dft_stage1",
    )(y.reshape(nb, FN_GROUPS, 2 * n1, cols), fc["m1"], fc["tw_c"], fc["tw_s"])

    kb = _tile(n1, 8, 1)
    out = pl.pallas_call(
        functools.partial(_fn_stage2_body, kb=kb, norm=1.0 / math.sqrt(seq_len * gd)),
        grid=(nb, n1 // kb),
        in_specs=[pl.BlockSpec((None, FN_GROUPS, 2, kb, n2, gd), lambda b, i: (b, 0, 0, i, 0, 0)),
                  pl.BlockSpec((n2, n2), lambda b, i: (0, 0)),
                  pl.BlockSpec((n2, n2), lambda b, i: (0, 0))],
        out_specs=pl.BlockSpec((None, n2, kb * MIX), lambda b, i: (b, 0, i)),
        out_shape=jax.ShapeDtypeStruct((nb, n2, n1 * MIX), BF16),
        compiler_params=_cp(("parallel", "parallel")),
        name="fnet_seq_dft_stage2",
    )(a.reshape(nb, FN_GROUPS, 2, n1, n2, gd), fc["c2"], fc["s2"])
    return out.reshape(nb * seq_len, MIX)


def _conv_body(ap_ref, ac_ref, an_ref, gp_ref, gc_ref, gn_ref, w_ref, b_ref, lg_ref, lb_ref, o_ref, z_ref,
               zs_ref, *, tl, nl):
    i = pl.program_id(1)
    glu = lambda a_ref, g_ref: a_ref[...].astype(F32) * jax.nn.sigmoid(g_ref[...].astype(F32))
    z_ref[0:CONV_HALO, :] = jnp.where(i > 0, glu(ap_ref, gp_ref), 0.0)
    z_ref[CONV_HALO:CONV_HALO + tl, :] = glu(ac_ref, gc_ref)
    z_ref[CONV_HALO + tl:2 * CONV_HALO + tl, :] = jnp.where(i < nl - 1, glu(an_ref, gn_ref), 0.0)
    span = zs_ref.shape[1]
    for s in range(SUBLANES):
        zs_ref[s] = z_ref[s:s + span, :]
    acc = jnp.broadcast_to(b_ref[...], (tl, MIX))
    first = CONV_HALO - CONV_TAPS // 2
    for j in range(CONV_TAPS):
        a, s = divmod(first + j, SUBLANES)
        acc = acc + w_ref[j:j + 1, :] * zs_ref[s, a * SUBLANES:a * SUBLANES + tl, :]
    mu = jnp.mean(acc, axis=-1, keepdims=True)
    xc = acc - mu
    var = jnp.mean(xc * xc, axis=-1, keepdims=True)
    y = xc * lax.rsqrt(var + LN_EPS) * lg_ref[...] + lb_ref[...]
    o_ref[...] = jax.nn.silu(y).astype(o_ref.dtype)


def _conformer_conv(u, row0, nb, seq_len, w_dw, b_dw, ln_g, ln_b):
    m = u.shape[0]
    tl = _tile(seq_len, 256, CONV_HALO)
    nl = seq_len // tl
    r0 = row0 // tl
    hb = tl // CONV_HALO
    last_hb = m // CONV_HALO - 1
    ca, cg = COL_CV // MIX, COL_CV // MIX + 1

    def cur(c):
        return pl.BlockSpec((tl, MIX), lambda b, i: (r0 + b * nl + i, c))

    def prev(c):
        return pl.BlockSpec((CONV_HALO, MIX), lambda b, i: (jnp.maximum((r0 + b * nl + i) * hb - 1, 0), c))

    def nxt(c):
        return pl.BlockSpec((CONV_HALO, MIX), lambda b, i: (jnp.minimum((r0 + b * nl + i + 1) * hb, last_hb), c))

    vec = pl.BlockSpec((1, MIX), lambda b, i: (0, 0))
    return pl.pallas_call(
        functools.partial(_conv_body, tl=tl, nl=nl),
        grid=(nb, nl),
        in_specs=[prev(ca), cur(ca), nxt(ca), prev(cg), cur(cg), nxt(cg),
                  pl.BlockSpec((CONV_TAPS, MIX), lambda b, i: (0, 0)), vec, vec, vec],
        out_specs=pl.BlockSpec((tl, MIX), lambda b, i: (b * nl + i, 0)),
        out_shape=jax.ShapeDtypeStruct((nb * seq_len, MIX), BF16),
        scratch_shapes=[pltpu.VMEM((tl + 2 * CONV_HALO, MIX), F32),
                        pltpu.VMEM((SUBLANES, tl + 2 * CONV_HALO - SUBLANES, MIX), F32)],
        compiler_params=_cp(("parallel", "arbitrary")),
        name="conformer_conv",
    )(u, u, u, u, u, u, w_dw, b_dw.reshape(1, MIX), ln_g.reshape(1, MIX), ln_b.reshape(1, MIX))


def _log_sigmoid(z):
    return jnp.minimum(z, 0.0) - jnp.log1p(jnp.exp(-jnp.abs(z)))


def _prefix_sum_rows(x):
    n = x.shape[0]
    row = lax.broadcasted_iota(jnp.int32, x.shape, 0)
    shift = 1
    while shift < n:
        x = x + jnp.where(row >= shift, pltpu.roll(x, shift, 0), 0.0)
        shift *= 2
    return x


def _gla_scan_body(qf_ref, kf_ref, vf_ref, gaf_ref, qb_ref, kb_ref, vb_ref, gab_ref, w2_ref, b2_ref,
                   of_ref, ob_ref, st_ref):
    @pl.when(pl.program_id(1) == 0)
    def _():
        st_ref[...] = jnp.zeros(st_ref.shape, F32)

    c = GLA_CHUNK
    row = lax.broadcasted_iota(jnp.int32, (c, c), 0)
    col = lax.broadcasted_iota(jnp.int32, (c, c), 1)
    kw = GLA_HEADS * GLA_DK
    dirs = ((qf_ref, kf_ref, vf_ref, gaf_ref, of_ref, col <= row),
            (qb_ref, kb_ref, vb_ref, gab_ref, ob_ref, col >= row))
    n_sub = qf_ref.shape[0] // c
    for step in range(n_sub):
        for d, (q_ref, k_ref, v_ref, ga_ref, o_ref, keep) in enumerate(dirs):
            sub = step if d == 0 else n_sub - 1 - step
            rows = slice(sub * c, (sub + 1) * c)
            z = jnp.dot(ga_ref[rows, :], w2_ref[:, d * kw:(d + 1) * kw], preferred_element_type=F32)
            g_all = _log_sigmoid(z + b2_ref[:, d * kw:(d + 1) * kw]) / GLA_TAU
            prefix = _prefix_sum_rows(g_all)
            tot_all = prefix[c - 1:c, :]
            b_all = prefix if d == 0 else tot_all - prefix + g_all
            for h in range(GLA_HEADS):
                ks = slice(h * GLA_DK, (h + 1) * GLA_DK)
                vs = slice(h * GLA_DV, (h + 1) * GLA_DV)
                b = b_all[:, ks]
                b_tot = tot_all[:, ks]
                q = q_ref[rows, ks].astype(F32) * (GLA_DK ** -0.5)
                k = k_ref[rows, ks].astype(F32)
                v = v_ref[rows, vs]
                q_dec = (q * jnp.exp(b)).astype(BF16)
                k_inv = (k * jnp.exp(-b)).astype(BF16)
                k_end = (k * jnp.exp(b_tot - b)).astype(BF16)
                st = st_ref[d, h]
                o_inter = lax.dot_general(q_dec, st.astype(BF16), (((1,), (1,)), ((), ())),
                                          preferred_element_type=F32)
                a = lax.dot_general(q_dec, k_inv, (((1,), (1,)), ((), ())), preferred_element_type=F32)
                a = jnp.where(keep, a, 0.0)
                o_intra = jnp.dot(a.astype(BF16), v, preferred_element_type=F32)
                o_ref[rows, vs] = o_inter + o_intra
                st_ref[d, h] = st * jnp.exp(b_tot) + lax.dot_general(
                    v, k_end, (((0,), (0,)), ((), ())), preferred_element_type=F32)


def _gla_post_body(of_ref, ob_ref, r_ref, g_ref, o_ref):
    g = g_ref[...]
    for h in range(GLA_HEADS):
        vs = slice(h * GLA_DV, (h + 1) * GLA_DV)
        o = _rms(of_ref[:, vs] + ob_ref[:, vs], g)
        o_ref[:, vs] = (o * jax.nn.silu(r_ref[:, vs].astype(F32))).astype(o_ref.dtype)


def _gla(u, ga, row0, nb, seq_len, w2, b2, g_norm):
    c = GLA_CHUNK * GLA_CHUNKS_PER_STEP
    assert seq_len % c == 0 and row0 % c == 0
    n = seq_len // c
    r0 = row0 // c
    kw = GLA_HEADS * GLA_DK
    fwd = lambda col: (lambda b, i: (r0 + b * n + i, col))
    bwd = lambda col: (lambda b, i: (r0 + b * n + n - 1 - i, col))
    specs = []
    for imap in (fwd, bwd):
        specs += [pl.BlockSpec((c, kw), imap(COL_GQ // kw)), pl.BlockSpec((c, kw), imap(COL_GK // kw)),
                  pl.BlockSpec((c, MIX), imap(COL_GV // MIX)), pl.BlockSpec((c, LANE), imap(0))]
    specs += [pl.BlockSpec((LANE, 2 * kw), lambda b, i: (0, 0)), pl.BlockSpec((1, 2 * kw), lambda b, i: (0, 0))]
    o_shape = jax.ShapeDtypeStruct((nb * seq_len, MIX), F32)
    o_f, o_b = pl.pallas_call(
        _gla_scan_body,
        grid=(nb, n),
        in_specs=specs,
        out_specs=[pl.BlockSpec((c, MIX), lambda b, i: (b * n + i, 0)),
                   pl.BlockSpec((c, MIX), lambda b, i: (b * n + n - 1 - i, 0))],
        out_shape=[o_shape, o_shape],
        scratch_shapes=[pltpu.VMEM((2, GLA_HEADS, GLA_DV, GLA_DK), F32)],
        compiler_params=_cp(("parallel", "arbitrary")),
        name="gla_scan",
    )(u, u, u, ga, u, u, u, ga, w2, b2)

    tl = _tile(seq_len, 256, 16)
    nl = seq_len // tl
    rr = row0 // tl
    row_spec = pl.BlockSpec((tl, MIX), lambda i: (i, 0))
    return pl.pallas_call(
        _gla_post_body,
        grid=(nb * nl,),
        in_specs=[row_spec, row_spec, pl.BlockSpec((tl, MIX), lambda i: (rr + i, COL_GR // MIX)),
                  pl.BlockSpec((1, GLA_DV), lambda i: (0, 0))],
        out_specs=row_spec,
        out_shape=jax.ShapeDtypeStruct((nb * seq_len, MIX), BF16),
        compiler_params=_cp(("parallel",)),
        name="gla_post",
    )(o_f, o_b, u, g_norm.reshape(1, GLA_DV))


def _merge_body(h_ref, wg_ref, bg_ref, z_ref, wb_ref, o_ref, acc_ref, *, n_branch):
    i = pl.program_id(2)
    gate = jax.nn.sigmoid(jnp.dot(h_ref[...], wg_ref[...], preferred_element_type=F32) + bg_ref[...])
    contrib = gate * jnp.dot(z_ref[...], wb_ref[...], preferred_element_type=F32)

    @pl.when(i == 0)
    def _():
        acc_ref[...] = contrib

    @pl.when(i > 0)
    def _():
        acc_ref[...] += contrib

    @pl.when(i == n_branch - 1)
    def _():
        o_ref[...] = acc_ref[...].astype(o_ref.dtype)


def _merge(h, z, w_gate, b_gate, w_branch, layer, tm=1024, tn=512):
    m, d = h.shape
    n_branch = z.shape[0]
    tm, tn = _tile(m, tm), _tile(d, tn, LANE)
    return pl.pallas_call(
        functools.partial(_merge_body, n_branch=n_branch),
        grid=(m // tm, d // tn, n_branch),
        in_specs=[pl.BlockSpec((tm, d), lambda a, j, i: (a, 0)),
                  pl.BlockSpec((None, None, d, tn), lambda a, j, i: (layer, i, 0, j)),
                  pl.BlockSpec((None, None, 1, tn), lambda a, j, i: (layer, i, 0, j)),
                  pl.BlockSpec((None, tm, MIX), lambda a, j, i: (i, a, 0)),
                  pl.BlockSpec((None, None, MIX, tn), lambda a, j, i: (layer, i, 0, j))],
        out_specs=pl.BlockSpec((tm, tn), lambda a, j, i: (a, j)),
        out_shape=jax.ShapeDtypeStruct((m, d), BF16),
        scratch_shapes=[pltpu.VMEM((tm, tn), F32)],
        compiler_params=_cp(("parallel", "parallel", "arbitrary")),
        name="gated_merge",
    )(h, w_gate, b_gate.reshape(b_gate.shape[0], n_branch, 1, d), z, w_branch)


def _router_body(x_ref, g_ref, wr_ref, h_ref, gate_ref, *, n_experts):
    hb = _rms(x_ref[...], g_ref[...]).astype(BF16)
    h_ref[...] = hb
    logits = jnp.dot(hb, wr_ref[...], preferred_element_type=F32)
    lane = lax.broadcasted_iota(jnp.int32, logits.shape, 1).astype(F32)
    neg = jnp.float32(-jnp.inf)
    big = jnp.float32(LANE)
    lg = jnp.where(lane < n_experts, logits, neg)
    m1 = jnp.max(lg, axis=-1, keepdims=True)
    i1 = jnp.min(jnp.where(lg == m1, lane, big), axis=-1, keepdims=True)
    lg2 = jnp.where(lane == i1, neg, lg)
    m2 = jnp.max(lg2, axis=-1, keepdims=True)
    i2 = jnp.min(jnp.where(lg2 == m2, lane, big), axis=-1, keepdims=True)
    e2 = jnp.exp(m2 - m1)
    denom = 1.0 + e2
    gate_ref[...] = jnp.where(lane == i1, 1.0 / denom, 0.0) + jnp.where(lane == i2, e2 / denom, 0.0)


def _norm_router(x, g, w_router_pad, n_experts):
    m, d = x.shape
    tm = _tile(m, 256)
    return pl.pallas_call(
        functools.partial(_router_body, n_experts=n_experts),
        grid=(m // tm,),
        in_specs=[pl.BlockSpec((tm, d), lambda i: (i, 0)),
                  pl.BlockSpec((1, d), lambda i: (0, 0)),
                  pl.BlockSpec((d, LANE), lambda i: (0, 0))],
        out_specs=[pl.BlockSpec((tm, d), lambda i: (i, 0)), pl.BlockSpec((tm, LANE), lambda i: (i, 0))],
        out_shape=[jax.ShapeDtypeStruct((m, d), BF16), jax.ShapeDtypeStruct((m, LANE), F32)],
        compiler_params=_cp(("parallel",)),
        name="norm_router",
    )(x, g.reshape(1, d), w_router_pad)


def kernel(x_prompt, x_sample, norm_mix, w_in, da_lambda, da_subln_g, cv_dw_w, cv_dw_b, cv_ln_g, cv_ln_b,
           gla_w_a2, gla_b_a2, gla_norm_g, w_gate, b_gate, w_branch, w_out, norm_ffn, ffn_w_gate, ffn_w_up,
           ffn_w_down, moe_router, moe_w_gate, moe_w_up, moe_w_down, norm_final):
    depth = norm_mix.shape[0]
    d = x_prompt.shape[-1]
    seqs = []
    row = 0
    for xs in (x_prompt, x_sample):
        seqs.append((row, xs.shape[0], xs.shape[1]))
        row += xs.shape[0] * xs.shape[1]
    x = jnp.concatenate([x_prompt.reshape(-1, d), x_sample.reshape(-1, d)], axis=0)

    w_in_b = w_in.astype(BF16)
    w_ga_b = jnp.pad(w_in[:, :, COL_GA:], ((0, 0), (0, 0), (0, LANE - 2 * GLA_RANK))).astype(BF16)
    w_gate_b = w_gate.astype(BF16)
    w_branch_b = w_branch.astype(BF16)
    w_out_b = w_out.astype(BF16)
    d_ff = ffn_w_gate.shape[-1]
    f_pad = _round_up(d_ff, 1024 if d_ff >= 1024 else LANE) - d_ff
    ffn_wg_b = jnp.pad(ffn_w_gate, ((0, 0), (0, 0), (0, f_pad))).astype(BF16)
    ffn_wu_b = jnp.pad(ffn_w_up, ((0, 0), (0, 0), (0, f_pad))).astype(BF16)
    ffn_wd_b = jnp.pad(ffn_w_down, ((0, 0), (0, f_pad), (0, 0))).astype(BF16)
    n_experts, f_exp = moe_w_gate.shape[1], moe_w_gate.shape[-1]
    moe_wg_b = moe_w_gate.astype(BF16)
    moe_wu_b = moe_w_up.astype(BF16)
    moe_wd_b = moe_w_down.astype(BF16).reshape(moe_w_down.shape[0], n_experts * f_exp, d)
    router_b = jnp.pad(moe_router, ((0, 0), (0, 0), (0, LANE - n_experts))).astype(BF16)
    kw = GLA_HEADS * GLA_DK
    w2 = jnp.zeros((depth, LANE, 2 * kw), F32)
    w2 = w2.at[:, :GLA_RANK, :kw].set(gla_w_a2[:, 0]).at[:, GLA_RANK:2 * GLA_RANK, kw:].set(gla_w_a2[:, 1])
    w2 = w2.astype(BF16)
    b2 = gla_b_a2.reshape(depth, 1, 2 * kw)

    rope_tabs = {s[2]: _rope_tables(s[2]) for s in seqs}
    fn_consts = {s[2]: _fourier_consts(s[2]) for s in seqs}

    for l in range(depth):
        h = _rmsnorm(x, norm_mix[l], BF16)
        u = _mm(h, w_in_b, (l,), COL_GA, BF16, name="in_proj")
        ga = _mm(h, w_ga_b, (l,), LANE, BF16, name="gla_gate_proj")
        lam_init = 0.8 - 0.6 * math.exp(-0.3 * l)
        lp = da_lambda[l].astype(F32)
        lam = jnp.exp(jnp.sum(lp[0] * lp[1])) - jnp.exp(jnp.sum(lp[2] * lp[3])) + lam_init
        zs = [[], [], [], []]
        for row0, nb, seq_len in seqs:
            qk = _rope(u, row0, nb, seq_len, rope_tabs[seq_len])
            zs[0].append(_attention(qk, u, row0, nb, seq_len, lam, da_subln_g[l], lam_init))
            zs[1].append(_fourier_mix(u, row0, nb, seq_len, fn_consts[seq_len]))
            zs[2].append(_conformer_conv(u, row0, nb, seq_len, cv_dw_w[l], cv_dw_b[l], cv_ln_g[l], cv_ln_b[l]))
            zs[3].append(_gla(u, ga, row0, nb, seq_len, w2[l], b2[l], gla_norm_g[l]))
        z = jnp.stack([jnp.concatenate(parts, axis=0) for parts in zs], axis=0)
        merged = _merge(h, z, w_gate_b, b_gate, w_branch_b, l)
        x = _mm(merged, w_out_b, (l,), d, F32, res=x, tn=512, name="out_proj")
        j = l // 2
        if l % 2 == 0:
            h2 = _rmsnorm(x, norm_ffn[l], BF16)
            mid = _glu_up(h2, ffn_wg_b, ffn_wu_b, (j,), 1, d_ff + f_pad)
            x = _mm(mid, ffn_wd_b, (j,), d, F32, res=x, tk=(d_ff + f_pad) // 4, name="ffn_down")
        else:
            h2, gates = _norm_router(x, norm_ffn[l], router_b[j], n_experts)
            mid = _glu_up(h2, moe_wg_b, moe_wu_b, (j,), n_experts, f_exp, gates=gates)
            x = _mm(mid, moe_wd_b, (j,), d, F32, res=x, tk=n_experts * f_exp // 4, name="moe_down")
    outs = [_rmsnorm(x, norm_final, F32, row0, nb * seq_len) for row0, nb, seq_len in seqs]
    return outs[0].reshape(x_prompt.shape), outs[1].reshape(x_sample.shape)
```

```python
import functools
import math

import jax
import jax.numpy as jnp
from jax import lax
from jax.experimental import pallas as pl
from jax.experimental.pallas import tpu as pltpu

F32 = jnp.float32
BF16 = jnp.bfloat16

MIX = 1024
DA_HEADS = 8
DA_HEAD_DIM = 64
DA_V_DIM = 128
ROPE_DIM = 16
ROPE_THETA = 500000.0
FN_GROUPS = 4
FN_GROUP_DIM = 256
CONV_TAPS = 31
CONV_HALO = 16
GLA_HEADS = 4
GLA_DK = 128
GLA_DV = 256
GLA_RANK = 16
GLA_TAU = 16.0
GLA_CHUNK = 64
GLA_CHUNKS_PER_STEP = 4
TOP_K = 2
RMS_EPS = 1e-6
LN_EPS = 1e-5

COL_Q = 0
COL_K = 1024
COL_V = 2048
COL_FN = 3072
COL_CV = 4096
COL_GQ = 6144
COL_GK = 6656
COL_GV = 7168
COL_GR = 8192
COL_GA = 9216
LANE = 128
SUBLANES = 8

VMEM_LIMIT_MB = 56


def _cp(sem, vmem_mb=VMEM_LIMIT_MB):
    return pltpu.CompilerParams(dimension_semantics=sem, vmem_limit_bytes=vmem_mb * 2**20)


def _tile(n, pref, align=8):
    if n <= pref:
        return n
    t = (pref // align) * align
    while t >= align:
        if n % t == 0:
            return t
        t -= align
    raise ValueError(f"no tile for {n} (pref {pref}, align {align})")


def _round_up(n, m):
    return (n + m - 1) // m * m


def _rms(x, g):
    ms = jnp.mean(x * x, axis=-1, keepdims=True)
    return x * lax.rsqrt(ms + RMS_EPS) * g


def _rmsnorm_body(x_ref, g_ref, o_ref):
    o_ref[...] = _rms(x_ref[...], g_ref[...]).astype(o_ref.dtype)


def _rmsnorm(x, g, out_dtype, row0=0, rows=None):
    d = x.shape[1]
    m = x.shape[0] if rows is None else rows
    tm = math.gcd(_tile(m, 256), row0) if row0 else _tile(m, 256)
    r0 = row0 // tm
    return pl.pallas_call(
        _rmsnorm_body,
        grid=(m // tm,),
        in_specs=[pl.BlockSpec((tm, d), lambda i: (r0 + i, 0)),
                  pl.BlockSpec((1, d), lambda i: (0, 0))],
        out_specs=pl.BlockSpec((tm, d), lambda i: (i, 0)),
        out_shape=jax.ShapeDtypeStruct((m, d), out_dtype),
        compiler_params=_cp(("parallel",)),
        name="rmsnorm",
    )(x, g.reshape(1, d))


def _mm_body(*refs, nk, has_res):
    if has_res:
        a_ref, w_ref, r_ref, o_ref = refs
    else:
        a_ref, w_ref, o_ref = refs
        r_ref = None

    def first():
        r = jnp.dot(a_ref[...], w_ref[...], preferred_element_type=F32)
        if has_res:
            r = r + r_ref[...]
        o_ref[...] = r.astype(o_ref.dtype)

    if nk == 1:
        first()
        return
    k = pl.program_id(2)
    pl.when(k == 0)(first)

    @pl.when(k > 0)
    def _():
        o_ref[...] += jnp.dot(a_ref[...], w_ref[...], preferred_element_type=F32)


def _mm(a, w, lead, n, out_dtype, res=None, tm=1024, tn=1024, tk=None, name="mm"):
    m, kdim = a.shape
    tm, tn = _tile(m, tm), _tile(n, tn, LANE)
    tk = kdim if tk is None else _tile(kdim, tk, LANE)
    nk = kdim // tk
    assert nk == 1 or out_dtype == F32
    nlead = len(lead)
    in_specs = [pl.BlockSpec((tm, tk), lambda i, j, k: (i, k)),
                pl.BlockSpec((None,) * nlead + (tk, tn), lambda i, j, k: lead + (k, j))]
    args = [a, w]
    if res is not None:
        in_specs.append(pl.BlockSpec((tm, tn), lambda i, j, k: (i, j)))
        args.append(res)
    return pl.pallas_call(
        functools.partial(_mm_body, nk=nk, has_res=res is not None),
        grid=(m // tm, n // tn, nk),
        in_specs=in_specs,
        out_specs=pl.BlockSpec((tm, tn), lambda i, j, k: (i, j)),
        out_shape=jax.ShapeDtypeStruct((m, n), out_dtype),
        compiler_params=_cp(("parallel", "parallel", "arbitrary")),
        name=name,
    )(*args)


def _glu_body(*refs, has_gate, blocks_per_expert):
    if has_gate:
        a_ref, wg_ref, wu_ref, gate_ref, o_ref = refs
    else:
        a_ref, wg_ref, wu_ref, o_ref = refs
        gate_ref = None
    a = a_ref[...]
    hg = jnp.dot(a, wg_ref[...], preferred_element_type=F32)
    hu = jnp.dot(a, wu_ref[...], preferred_element_type=F32)
    h = jax.nn.silu(hg) * hu
    if has_gate:
        e = (pl.program_id(1) // blocks_per_expert).astype(F32)
        gates = gate_ref[...]
        lane = lax.broadcasted_iota(jnp.int32, gates.shape, 1).astype(F32)
        h = h * jnp.sum(jnp.where(lane == e, gates, 0.0), axis=-1, keepdims=True)
    o_ref[...] = h.astype(o_ref.dtype)


def _glu_up(a, wg, wu, lead, n_experts, f, gates=None, tm=1024, tn=512):
    m, kdim = a.shape
    tm = _tile(m, tm)
    tn = tn if f % tn == 0 else _tile(f, tn // 2, LANE)
    bpe = f // tn
    nlead = len(lead)
    if gates is not None:
        w_spec = pl.BlockSpec((None,) * (nlead + 1) + (kdim, tn), lambda i, j: lead + (j // bpe, 0, j % bpe))
    else:
        w_spec = pl.BlockSpec((None,) * nlead + (kdim, tn), lambda i, j: lead + (0, j))
    in_specs = [pl.BlockSpec((tm, kdim), lambda i, j: (i, 0)), w_spec, w_spec]
    args = [a, wg, wu]
    if gates is not None:
        in_specs.append(pl.BlockSpec((tm, LANE), lambda i, j: (i, 0)))
        args.append(gates)
    return pl.pallas_call(
        functools.partial(_glu_body, has_gate=gates is not None, blocks_per_expert=bpe),
        grid=(m // tm, n_experts * bpe),
        in_specs=in_specs,
        out_specs=pl.BlockSpec((tm, tn), lambda i, j: (i, j)),
        out_shape=jax.ShapeDtypeStruct((m, n_experts * f), BF16),
        compiler_params=_cp(("parallel", "arbitrary")),
        name="glu_up",
    )(*args)


def _rope_tables(seq_len):
    half = ROPE_DIM // 2
    inv_freq = 1.0 / (ROPE_THETA ** (jnp.arange(half, dtype=F32) * 2.0 / ROPE_DIM))
    ang = jnp.arange(seq_len, dtype=F32)[:, None] * inv_freq[None, :]
    cos, sin = jnp.cos(ang), jnp.sin(ang)
    one = jnp.ones((seq_len, DA_HEAD_DIM - ROPE_DIM), F32)
    zero = jnp.zeros((seq_len, DA_HEAD_DIM - ROPE_DIM), F32)
    zh = jnp.zeros_like(sin)
    comp = lambda a, b, rest: jnp.concatenate([a, b, rest], axis=-1)
    c = comp(cos, cos, one)
    s_from_lower = comp(zh, sin, zero)
    s_from_upper = comp(-sin, zh, zero)
    two = lambda t: jnp.concatenate([t, t], axis=-1)
    return two(c), two(s_from_lower), two(s_from_upper)


def _rope_body(x_ref, c_ref, sl_ref, su_ref, o_ref, *, q_scale):
    c, sl, su = c_ref[...], sl_ref[...], su_ref[...]
    n_blocks = 2 * DA_HEADS
    for j in range(n_blocks):
        t = x_ref[:, j * LANE:(j + 1) * LANE].astype(F32)
        r = t * c + pltpu.roll(t, ROPE_DIM // 2, 1) * sl + pltpu.roll(t, LANE - ROPE_DIM // 2, 1) * su
        if j < DA_HEADS:
            r = r * q_scale
        o_ref[:, j * LANE:(j + 1) * LANE] = r.astype(o_ref.dtype)


def _rope(u, row0, nb, seq_len, tables):
    tl = _tile(seq_len, 512, 16)
    nl = seq_len // tl
    r0 = row0 // tl
    width = 2 * MIX
    tab_spec = pl.BlockSpec((tl, LANE), lambda i: (i % nl, 0))
    return pl.pallas_call(
        functools.partial(_rope_body, q_scale=DA_HEAD_DIM ** -0.5),
        grid=(nb * nl,),
        in_specs=[pl.BlockSpec((tl, width), lambda i: (r0 + i, 0)), tab_spec, tab_spec, tab_spec],
        out_specs=pl.BlockSpec((tl, width), lambda i: (i, 0)),
        out_shape=jax.ShapeDtypeStruct((nb * seq_len, width), BF16),
        compiler_params=_cp(("parallel",)),
        name="rope",
    )(u, *tables)


def _attn_body(lam_ref, g_ref, q_ref, k_ref, v_ref, o_ref, s_ref, rm_ref, m_ref, l_ref, acc_ref,
               *, tk, nkv, post_scale):
    q = q_ref[...]
    lane = lax.broadcasted_iota(jnp.int32, q.shape, 1)
    zero = jnp.zeros_like(q)
    qs = (jnp.where(lane < DA_HEAD_DIM, q, zero), jnp.where(lane >= DA_HEAD_DIM, q, zero))
    nt = tk // LANE
    stat_shape = m_ref.shape[1:]

    def produce(slot, j):
        kk = k_ref[pl.ds(pl.multiple_of(j * tk, tk), tk), :]
        for c in range(2):
            s = lax.dot_general(qs[c], kk, (((1,), (1,)), ((), ())), preferred_element_type=F32)
            s_ref[slot, c] = s
            part = s[:, :LANE]
            for t in range(1, nt):
                part = jnp.maximum(part, s[:, t * LANE:(t + 1) * LANE])
            rm_ref[slot, c] = jnp.broadcast_to(jnp.max(part, axis=-1, keepdims=True), stat_shape)

    def consume(slot, j):
        vv = v_ref[pl.ds(pl.multiple_of(j * tk, tk), tk), :]
        for c in range(2):
            m_prev = m_ref[c]
            m_new = jnp.maximum(m_prev, rm_ref[slot, c])
            alpha = jnp.exp(m_prev - m_new)
            ps = [jnp.exp(s_ref[slot, c, :, t * LANE:(t + 1) * LANE] - m_new) for t in range(nt)]
            l_ref[c] = alpha * l_ref[c] + functools.reduce(lambda a, b: a + b, ps)
            p = jnp.concatenate(ps, axis=1).astype(BF16)
            acc_ref[c] = alpha * acc_ref[c] + jnp.dot(p, vv, preferred_element_type=F32)
            m_ref[c] = m_new

    m_ref[...] = jnp.full(m_ref.shape, -jnp.inf, F32)
    l_ref[...] = jnp.zeros(l_ref.shape, F32)
    acc_ref[...] = jnp.zeros(acc_ref.shape, F32)
    produce(0, 0)
    if nkv == 1:
        consume(0, 0)
    else:
        assert nkv % 2 == 0

        def pair(i, carry):
            j = 2 * i
            produce(1, j + 1)
            consume(0, j)
            produce(0, jnp.where(j + 2 < nkv, j + 2, 0))
            consume(1, j + 1)
            return carry

        lax.fori_loop(0, nkv // 2, pair, 0)
    l0 = jnp.sum(l_ref[0], axis=-1, keepdims=True)
    l1 = jnp.sum(l_ref[1], axis=-1, keepdims=True)
    o = acc_ref[0] / l0 - lam_ref[...] * (acc_ref[1] / l1)
    o_ref[...] = (_rms(o, g_ref[...]) * post_scale).astype(o_ref.dtype)


def _attention(qk, u, row0, nb, seq_len, lam, g_subln, lam_init):
    tq = _tile(seq_len, 1024, 16)
    tk = _tile(seq_len, 1024, LANE)
    nq = seq_len // tq
    assert row0 % seq_len == 0
    rb = row0 // seq_len
    return pl.pallas_call(
        functools.partial(_attn_body, tk=tk, nkv=seq_len // tk, post_scale=1.0 - lam_init),
        grid=(nb, DA_HEADS, nq),
        in_specs=[pl.BlockSpec((1, LANE), lambda b, h, i: (0, 0)),
                  pl.BlockSpec((1, LANE), lambda b, h, i: (0, 0)),
                  pl.BlockSpec((tq, LANE), lambda b, h, i: (b * nq + i, h)),
                  pl.BlockSpec((seq_len, LANE), lambda b, h, i: (b, DA_HEADS + h)),
                  pl.BlockSpec((seq_len, LANE), lambda b, h, i: (rb + b, COL_V // LANE + h))],
        out_specs=pl.BlockSpec((tq, LANE), lambda b, h, i: (b * nq + i, h)),
        out_shape=jax.ShapeDtypeStruct((nb * seq_len, MIX), BF16),
        scratch_shapes=[pltpu.VMEM((2, 2, tq, tk), F32), pltpu.VMEM((2, 2, tq, LANE), F32),
                        pltpu.VMEM((2, tq, LANE), F32), pltpu.VMEM((2, tq, LANE), F32),
                        pltpu.VMEM((2, tq, DA_V_DIM), F32)],
        compiler_params=_cp(("parallel", "parallel", "arbitrary")),
        name="diff_attention",
    )(jnp.full((1, LANE), lam, F32), g_subln.reshape(1, DA_V_DIM).astype(F32), qk, qk, u)


def _dft_cs(n):
    idx = jnp.arange(n, dtype=jnp.int32)
    ang = ((idx[:, None] * idx[None, :]) % n).astype(F32) * (2.0 * math.pi / n)
    return jnp.cos(ang), jnp.sin(ang)


def _fourier_consts(seq_len):
    n1 = 1 << ((seq_len.bit_length() - 1 + 1) // 2)
    n2 = seq_len // n1
    assert n1 * n2 == seq_len and n1 % 16 == 0 and n2 % 16 == 0
    cc, sc = _dft_cs(FN_GROUP_DIM)
    w_chan = jnp.concatenate([cc, -sc], axis=1).astype(BF16)
    c1, s1 = _dft_cs(n1)
    m1 = jnp.concatenate([jnp.concatenate([c1, s1], 1), jnp.concatenate([-s1, c1], 1)], 0).astype(BF16)
    k1 = jnp.arange(n1, dtype=jnp.int32)[:, None]
    i2 = jnp.arange(n2, dtype=jnp.int32)[None, :]
    ang = ((k1 * i2) % seq_len).astype(F32) * (2.0 * math.pi / seq_len)
    tw_cos, tw_sin = lax.optimization_barrier((jnp.cos(ang), jnp.sin(ang)))
    expand = lambda t: jnp.repeat(t, FN_GROUP_DIM, axis=1)
    c2, s2 = _dft_cs(n2)
    return dict(n1=n1, n2=n2, w_chan=w_chan, m1=m1, tw_c=expand(tw_cos), tw_s=expand(tw_sin),
                c2=c2.astype(BF16), s2=s2.astype(BF16))


def _fn_chan_body(x_ref, w_ref, o_ref):
    y = jnp.dot(x_ref[...], w_ref[...], preferred_element_type=F32)
    o_ref[0] = y[:, :FN_GROUP_DIM].astype(o_ref.dtype)
    o_ref[1] = y[:, FN_GROUP_DIM:].astype(o_ref.dtype)


def _fn_stage1_body(y_ref, m1_ref, tc_ref, ts_ref, o_ref, *, n1):
    a = jnp.dot(m1_ref[...], y_ref[...], preferred_element_type=F32)
    ar, ai = a[:n1], a[n1:]
    tc, ts = tc_ref[...], ts_ref[...]
    o_ref[:n1, :] = (ar * tc + ai * ts).astype(o_ref.dtype)
    o_ref[n1:, :] = (ai * tc - ar * ts).astype(o_ref.dtype)


def _fn_stage2_body(a_ref, c2_ref, s2_ref, o_ref, *, kb, norm):
    c2, s2 = c2_ref[...], s2_ref[...]
    for j in range(kb):
        for g in range(FN_GROUPS):
            r = (jnp.dot(c2, a_ref[g, 0, j], preferred_element_type=F32)
                 + jnp.dot(s2, a_ref[g, 1, j], preferred_element_type=F32))
            col = (j * FN_GROUPS + g) * FN_GROUP_DIM
            o_ref[:, col:col + FN_GROUP_DIM] = (r * norm).astype(o_ref.dtype)


def _fourier_mix(u, row0, nb, seq_len, fc):
    n1, n2 = fc["n1"], fc["n2"]
    gd = FN_GROUP_DIM
    tl = _tile(seq_len, 512, 16)
    nl = seq_len // tl
    r0 = row0 // tl
    y = pl.pallas_call(
        _fn_chan_body,
        grid=(nb, FN_GROUPS, nl),
        in_specs=[pl.BlockSpec((tl, gd), lambda b, g, i: (r0 + b * nl + i, COL_FN // gd + g)),
                  pl.BlockSpec((gd, 2 * gd), lambda b, g, i: (0, 0))],
        out_specs=pl.BlockSpec((None, None, 2, tl, gd), lambda b, g, i: (b, g, 0, i, 0)),
        out_shape=jax.ShapeDtypeStruct((nb, FN_GROUPS, 2, seq_len, gd), BF16),
        compiler_params=_cp(("parallel", "parallel", "parallel")),
        name="fnet_channel_dft",
    )(u, fc["w_chan"])

    cols = n2 * gd
    tc = _tile(cols, 2048, LANE)
    a = pl.pallas_call(
        functools.partial(_fn_stage1_body, n1=n1),
        grid=(nb, FN_GROUPS, cols // tc),
        in_specs=[pl.BlockSpec((None, None, 2 * n1, tc), lambda b, g, i: (b, g, 0, i)),
                  pl.BlockSpec((2 * n1, 2 * n1), lambda b, g, i: (0, 0)),
                  pl.BlockSpec((n1, tc), lambda b, g, i: (0, i)),
                  pl.BlockSpec((n1, tc), lambda b, g, i: (0, i))],
        out_specs=pl.BlockSpec((None, None, 2 * n1, tc), lambda b, g, i: (b, g, 0, i)),
        out_shape=jax.ShapeDtypeStruct((nb, FN_GROUPS, 2 * n1, cols), BF16),
        compiler_params=_cp(("parallel", "parallel", "parallel")),
        name="fnet_seq_dft_stage1",
    )(y.reshape(nb, FN_GROUPS, 2 * n1, cols), fc["m1"], fc["tw_c"], fc["tw_s"])

    kb = _tile(n1, 8, 1)
    out = pl.pallas_call(
        functools.partial(_fn_stage2_body, kb=kb, norm=1.0 / math.sqrt(seq_len * gd)),
        grid=(nb, n1 // kb),
        in_specs=[pl.BlockSpec((None, FN_GROUPS, 2, kb, n2, gd), lambda b, i: (b, 0, 0, i, 0, 0)),
                  pl.BlockSpec((n2, n2), lambda b, i: (0, 0)),
                  pl.BlockSpec((n2, n2), lambda b, i: (0, 0))],
        out_specs=pl.BlockSpec((None, n2, kb * MIX), lambda b, i: (b, 0, i)),
        out_shape=jax.ShapeDtypeStruct((nb, n2, n1 * MIX), BF16),
        compiler_params=_cp(("parallel", "parallel")),
        name="fnet_seq_dft_stage2",
    )(a.reshape(nb, FN_GROUPS, 2, n1, n2, gd), fc["c2"], fc["s2"])
    return out.reshape(nb * seq_len, MIX)


def _conv_body(ap_ref, ac_ref, an_ref, gp_ref, gc_ref, gn_ref, w_ref, b_ref, lg_ref, lb_ref, o_ref, z_ref,
               zs_ref, *, tl, nl):
    i = pl.program_id(1)
    glu = lambda a_ref, g_ref: a_ref[...].astype(F32) * jax.nn.sigmoid(g_ref[...].astype(F32))
    z_ref[0:CONV_HALO, :] = jnp.where(i > 0, glu(ap_ref, gp_ref), 0.0)
    z_ref[CONV_HALO:CONV_HALO + tl, :] = glu(ac_ref, gc_ref)
    z_ref[CONV_HALO + tl:2 * CONV_HALO + tl, :] = jnp.where(i < nl - 1, glu(an_ref, gn_ref), 0.0)
    span = zs_ref.shape[1]
    for s in range(SUBLANES):
        zs_ref[s] = z_ref[s:s + span, :]
    acc = jnp.broadcast_to(b_ref[...], (tl, MIX))
    first = CONV_HALO - CONV_TAPS // 2
    for j in range(CONV_TAPS):
        a, s = divmod(first + j, SUBLANES)
        acc = acc + w_ref[j:j + 1, :] * zs_ref[s, a * SUBLANES:a * SUBLANES + tl, :]
    mu = jnp.mean(acc, axis=-1, keepdims=True)
    xc = acc - mu
    var = jnp.mean(xc * xc, axis=-1, keepdims=True)
    y = xc * lax.rsqrt(var + LN_EPS) * lg_ref[...] + lb_ref[...]
    o_ref[...] = jax.nn.silu(y).astype(o_ref.dtype)


def _conformer_conv(u, row0, nb, seq_len, w_dw, b_dw, ln_g, ln_b):
    m = u.shape[0]
    tl = _tile(seq_len, 256, CONV_HALO)
    nl = seq_len // tl
    r0 = row0 // tl
    hb = tl // CONV_HALO
    last_hb = m // CONV_HALO - 1
    ca, cg = COL_CV // MIX, COL_CV // MIX + 1

    def cur(c):
        return pl.BlockSpec((tl, MIX), lambda b, i: (r0 + b * nl + i, c))

    def prev(c):
        return pl.BlockSpec((CONV_HALO, MIX), lambda b, i: (jnp.maximum((r0 + b * nl + i) * hb - 1, 0), c))

    def nxt(c):
        return pl.BlockSpec((CONV_HALO, MIX), lambda b, i: (jnp.minimum((r0 + b * nl + i + 1) * hb, last_hb), c))

    vec = pl.BlockSpec((1, MIX), lambda b, i: (0, 0))
    return pl.pallas_call(
        functools.partial(_conv_body, tl=tl, nl=nl),
        grid=(nb, nl),
        in_specs=[prev(ca), cur(ca), nxt(ca), prev(cg), cur(cg), nxt(cg),
                  pl.BlockSpec((CONV_TAPS, MIX), lambda b, i: (0, 0)), vec, vec, vec],
        out_specs=pl.BlockSpec((tl, MIX), lambda b, i: (b * nl + i, 0)),
        out_shape=jax.ShapeDtypeStruct((nb * seq_len, MIX), BF16),
        scratch_shapes=[pltpu.VMEM((tl + 2 * CONV_HALO, MIX), F32),
                        pltpu.VMEM((SUBLANES, tl + 2 * CONV_HALO - SUBLANES, MIX), F32)],
        compiler_params=_cp(("parallel", "arbitrary")),
        name="conformer_conv",
    )(u, u, u, u, u, u, w_dw, b_dw.reshape(1, MIX), ln_g.reshape(1, MIX), ln_b.reshape(1, MIX))


def _log_sigmoid(z):
    return jnp.minimum(z, 0.0) - jnp.log1p(jnp.exp(-jnp.abs(z)))


def _prefix_sum_rows(x):
    n = x.shape[0]
    row = lax.broadcasted_iota(jnp.int32, x.shape, 0)
    shift = 1
    while shift < n:
        x = x + jnp.where(row >= shift, pltpu.roll(x, shift, 0), 0.0)
        shift *= 2
    return x


def _gla_scan_body(qf_ref, kf_ref, vf_ref, gaf_ref, qb_ref, kb_ref, vb_ref, gab_ref, w2_ref, b2_ref,
                   of_ref, ob_ref, st_ref):
    @pl.when(pl.program_id(1) == 0)
    def _():
        st_ref[...] = jnp.zeros(st_ref.shape, F32)

    c = GLA_CHUNK
    row = lax.broadcasted_iota(jnp.int32, (c, c), 0)
    col = lax.broadcasted_iota(jnp.int32, (c, c), 1)
    kw = GLA_HEADS * GLA_DK
    dirs = ((qf_ref, kf_ref, vf_ref, gaf_ref, of_ref, col <= row),
            (qb_ref, kb_ref, vb_ref, gab_ref, ob_ref, col >= row))
    n_sub = qf_ref.shape[0] // c
    for step in range(n_sub):
        for d, (q_ref, k_ref, v_ref, ga_ref, o_ref, keep) in enumerate(dirs):
            sub = step if d == 0 else n_sub - 1 - step
            rows = slice(sub * c, (sub + 1) * c)
            z = jnp.dot(ga_ref[rows, :], w2_ref[:, d * kw:(d + 1) * kw], preferred_element_type=F32)
            g_all = _log_sigmoid(z + b2_ref[:, d * kw:(d + 1) * kw]) / GLA_TAU
            prefix = _prefix_sum_rows(g_all)
            tot_all = prefix[c - 1:c, :]
            b_all = prefix if d == 0 else tot_all - prefix + g_all
            for h in range(GLA_HEADS):
                ks = slice(h * GLA_DK, (h + 1) * GLA_DK)
                vs = slice(h * GLA_DV, (h + 1) * GLA_DV)
                b = b_all[:, ks]
                b_tot = tot_all[:, ks]
                q = q_ref[rows, ks].astype(F32) * (GLA_DK ** -0.5)
                k = k_ref[rows, ks].astype(F32)
                v = v_ref[rows, vs]
                q_dec = (q * jnp.exp(b)).astype(BF16)
                k_inv = (k * jnp.exp(-b)).astype(BF16)
                k_end = (k * jnp.exp(b_tot - b)).astype(BF16)
                st = st_ref[d, h]
                o_inter = lax.dot_general(q_dec, st.astype(BF16), (((1,), (1,)), ((), ())),
                                          preferred_element_type=F32)
                a = lax.dot_general(q_dec, k_inv, (((1,), (1,)), ((), ())), preferred_element_type=F32)
                a = jnp.where(keep, a, 0.0)
                o_intra = jnp.dot(a.astype(BF16), v, preferred_element_type=F32)
                o_ref[rows, vs] = o_inter + o_intra
                st_ref[d, h] = st * jnp.exp(b_tot) + lax.dot_general(
                    v, k_end, (((0,), (0,)), ((), ())), preferred_element_type=F32)


def _gla_post_body(of_ref, ob_ref, r_ref, g_ref, o_ref):
    g = g_ref[...]
    for h in range(GLA_HEADS):
        vs = slice(h * GLA_DV, (h + 1) * GLA_DV)
        o = _rms(of_ref[:, vs] + ob_ref[:, vs], g)
        o_ref[:, vs] = (o * jax.nn.silu(r_ref[:, vs].astype(F32))).astype(o_ref.dtype)


def _gla(u, ga, row0, nb, seq_len, w2, b2, g_norm):
    c = GLA_CHUNK * GLA_CHUNKS_PER_STEP
    assert seq_len % c == 0 and row0 % c == 0
    n = seq_len // c
    r0 = row0 // c
    kw = GLA_HEADS * GLA_DK
    fwd = lambda col: (lambda b, i: (r0 + b * n + i, col))
    bwd = lambda col: (lambda b, i: (r0 + b * n + n - 1 - i, col))
    specs = []
    for imap in (fwd, bwd):
        specs += [pl.BlockSpec((c, kw), imap(COL_GQ // kw)), pl.BlockSpec((c, kw), imap(COL_GK // kw)),
                  pl.BlockSpec((c, MIX), imap(COL_GV // MIX)), pl.BlockSpec((c, LANE), imap(0))]
    specs += [pl.BlockSpec((LANE, 2 * kw), lambda b, i: (0, 0)), pl.BlockSpec((1, 2 * kw), lambda b, i: (0, 0))]
    o_shape = jax.ShapeDtypeStruct((nb * seq_len, MIX), F32)
    o_f, o_b = pl.pallas_call(
        _gla_scan_body,
        grid=(nb, n),
        in_specs=specs,
        out_specs=[pl.BlockSpec((c, MIX), lambda b, i: (b * n + i, 0)),
                   pl.BlockSpec((c, MIX), lambda b, i: (b * n + n - 1 - i, 0))],
        out_shape=[o_shape, o_shape],
        scratch_shapes=[pltpu.VMEM((2, GLA_HEADS, GLA_DV, GLA_DK), F32)],
        compiler_params=_cp(("parallel", "arbitrary")),
        name="gla_scan",
    )(u, u, u, ga, u, u, u, ga, w2, b2)

    tl = _tile(seq_len, 256, 16)
    nl = seq_len // tl
    rr = row0 // tl
    row_spec = pl.BlockSpec((tl, MIX), lambda i: (i, 0))
    return pl.pallas_call(
        _gla_post_body,
        grid=(nb * nl,),
        in_specs=[row_spec, row_spec, pl.BlockSpec((tl, MIX), lambda i: (rr + i, COL_GR // MIX)),
                  pl.BlockSpec((1, GLA_DV), lambda i: (0, 0))],
        out_specs=row_spec,
        out_shape=jax.ShapeDtypeStruct((nb * seq_len, MIX), BF16),
        compiler_params=_cp(("parallel",)),
        name="gla_post",
    )(o_f, o_b, u, g_norm.reshape(1, GLA_DV))


def _merge_body(h_ref, wg_ref, bg_ref, z_ref, wb_ref, o_ref, acc_ref, *, n_branch):
    i = pl.program_id(2)
    gate = jax.nn.sigmoid(jnp.dot(h_ref[...], wg_ref[...], preferred_element_type=F32) + bg_ref[...])
    contrib = gate * jnp.dot(z_ref[...], wb_ref[...], preferred_element_type=F32)

    @pl.when(i == 0)
    def _():
        acc_ref[...] = contrib

    @pl.when(i > 0)
    def _():
        acc_ref[...] += contrib

    @pl.when(i == n_branch - 1)
    def _():
        o_ref[...] = acc_ref[...].astype(o_ref.dtype)


def _merge(h, z, w_gate, b_gate, w_branch, layer, tm=1024, tn=512):
    m, d = h.shape
    n_branch = z.shape[0]
    tm, tn = _tile(m, tm), _tile(d, tn, LANE)
    return pl.pallas_call(
        functools.partial(_merge_body, n_branch=n_branch),
        grid=(m // tm, d // tn, n_branch),
        in_specs=[pl.BlockSpec((tm, d), lambda a, j, i: (a, 0)),
                  pl.BlockSpec((None, None, d, tn), lambda a, j, i: (layer, i, 0, j)),
                  pl.BlockSpec((None, None, 1, tn), lambda a, j, i: (layer, i, 0, j)),
                  pl.BlockSpec((None, tm, MIX), lambda a, j, i: (i, a, 0)),
                  pl.BlockSpec((None, None, MIX, tn), lambda a, j, i: (layer, i, 0, j))],
        out_specs=pl.BlockSpec((tm, tn), lambda a, j, i: (a, j)),
        out_shape=jax.ShapeDtypeStruct((m, d), BF16),
        scratch_shapes=[pltpu.VMEM((tm, tn), F32)],
        compiler_params=_cp(("parallel", "parallel", "arbitrary")),
        name="gated_merge",
    )(h, w_gate, b_gate.reshape(b_gate.shape[0], n_branch, 1, d), z, w_branch)


def _router_body(x_ref, g_ref, wr_ref, h_ref, gate_ref, *, n_experts):
    hb = _rms(x_ref[...], g_ref[...]).astype(BF16)
    h_ref[...] = hb
    logits = jnp.dot(hb, wr_ref[...], preferred_element_type=F32)
    lane = lax.broadcasted_iota(jnp.int32, logits.shape, 1).astype(F32)
    neg = jnp.float32(-jnp.inf)
    big = jnp.float32(LANE)
    lg = jnp.where(lane < n_experts, logits, neg)
    m1 = jnp.max(lg, axis=-1, keepdims=True)
    i1 = jnp.min(jnp.where(lg == m1, lane, big), axis=-1, keepdims=True)
    lg2 = jnp.where(lane == i1, neg, lg)
    m2 = jnp.max(lg2, axis=-1, keepdims=True)
    i2 = jnp.min(jnp.where(lg2 == m2, lane, big), axis=-1, keepdims=True)
    e2 = jnp.exp(m2 - m1)
    denom = 1.0 + e2
    gate_ref[...] = jnp.where(lane == i1, 1.0 / denom, 0.0) + jnp.where(lane == i2, e2 / denom, 0.0)


def _norm_router(x, g, w_router_pad, n_experts):
    m, d = x.shape
    tm = _tile(m, 256)
    return pl.pallas_call(
        functools.partial(_router_body, n_experts=n_experts),
        grid=(m // tm,),
        in_specs=[pl.BlockSpec((tm, d), lambda i: (i, 0)),
                  pl.BlockSpec((1, d), lambda i: (0, 0)),
                  pl.BlockSpec((d, LANE), lambda i: (0, 0))],
        out_specs=[pl.BlockSpec((tm, d), lambda i: (i, 0)), pl.BlockSpec((tm, LANE), lambda i: (i, 0))],
        out_shape=[jax.ShapeDtypeStruct((m, d), BF16), jax.ShapeDtypeStruct((m, LANE), F32)],
        compiler_params=_cp(("parallel",)),
        name="norm_router",
    )(x, g.reshape(1, d), w_router_pad)


def kernel(x_prompt, x_sample, norm_mix, w_in, da_lambda, da_subln_g, cv_dw_w, cv_dw_b, cv_ln_g, cv_ln_b,
           gla_w_a2, gla_b_a2, gla_norm_g, w_gate, b_gate, w_branch, w_out, norm_ffn, ffn_w_gate, ffn_w_up,
           ffn_w_down, moe_router, moe_w_gate, moe_w_up, moe_w_down, norm_final):
    depth = norm_mix.shape[0]
    d = x_prompt.shape[-1]
    seqs = []
    row = 0
    for xs in (x_prompt, x_sample):
        seqs.append((row, xs.shape[0], xs.shape[1]))
        row += xs.shape[0] * xs.shape[1]
    x = jnp.concatenate([x_prompt.reshape(-1, d), x_sample.reshape(-1, d)], axis=0)

    w_in_b = w_in.astype(BF16)
    w_ga_b = jnp.pad(w_in[:, :, COL_GA:], ((0, 0), (0, 0), (0, LANE - 2 * GLA_RANK))).astype(BF16)
    w_gate_b = w_gate.astype(BF16)
    w_branch_b = w_branch.astype(BF16)
    w_out_b = w_out.astype(BF16)
    d_ff = ffn_w_gate.shape[-1]
    f_pad = _round_up(d_ff, 1024 if d_ff >= 1024 else LANE) - d_ff
    ffn_wg_b = jnp.pad(ffn_w_gate, ((0, 0), (0, 0), (0, f_pad))).astype(BF16)
    ffn_wu_b = jnp.pad(ffn_w_up, ((0, 0), (0, 0), (0, f_pad))).astype(BF16)
    ffn_wd_b = jnp.pad(ffn_w_down, ((0, 0), (0, f_pad), (0, 0))).astype(BF16)
    n_experts, f_exp = moe_w_gate.shape[1], moe_w_gate.shape[-1]
    moe_wg_b = moe_w_gate.astype(BF16)
    moe_wu_b = moe_w_up.astype(BF16)
    moe_wd_b = moe_w_down.astype(BF16).reshape(moe_w_down.shape[0], n_experts * f_exp, d)
    router_b = jnp.pad(moe_router, ((0, 0), (0, 0), (0, LANE - n_experts))).astype(BF16)
    kw = GLA_HEADS * GLA_DK
    w2 = jnp.zeros((depth, LANE, 2 * kw), F32)
    w2 = w2.at[:, :GLA_RANK, :kw].set(gla_w_a2[:, 0]).at[:, GLA_RANK:2 * GLA_RANK, kw:].set(gla_w_a2[:, 1])
    w2 = w2.astype(BF16)
    b2 = gla_b_a2.reshape(depth, 1, 2 * kw)

    rope_tabs = {s[2]: _rope_tables(s[2]) for s in seqs}
    fn_consts = {s[2]: _fourier_consts(s[2]) for s in seqs}

    for l in range(depth):
        h = _rmsnorm(x, norm_mix[l], BF16)
        u = _mm(h, w_in_b, (l,), COL_GA, BF16, name="in_proj")
        ga = _mm(h, w_ga_b, (l,), LANE, BF16, name="gla_gate_proj")
        lam_init = 0.8 - 0.6 * math.exp(-0.3 * l)
        lp = da_lambda[l].astype(F32)
        lam = jnp.exp(jnp.sum(lp[0] * lp[1])) - jnp.exp(jnp.sum(lp[2] * lp[3])) + lam_init
        zs = [[], [], [], []]
        for row0, nb, seq_len in seqs:
            qk = _rope(u, row0, nb, seq_len, rope_tabs[seq_len])
            zs[0].append(_attention(qk, u, row0, nb, seq_len, lam, da_subln_g[l], lam_init))
            zs[1].append(_fourier_mix(u, row0, nb, seq_len, fn_consts[seq_len]))
            zs[2].append(_conformer_conv(u, row0, nb, seq_len, cv_dw_w[l], cv_dw_b[l], cv_ln_g[l], cv_ln_b[l]))
            zs[3].append(_gla(u, ga, row0, nb, seq_len, w2[l], b2[l], gla_norm_g[l]))
        z = jnp.stack([jnp.concatenate(parts, axis=0) for parts in zs], axis=0)
        merged = _merge(h, z, w_gate_b, b_gate, w_branch_b, l)
        x = _mm(merged, w_out_b, (l,), d, F32, res=x, tn=512, name="out_proj")
        j = l // 2
        if l % 2 == 0:
            h2 = _rmsnorm(x, norm_ffn[l], BF16)
            mid = _glu_up(h2, ffn_wg_b, ffn_wu_b, (j,), 1, d_ff + f_pad)
            x = _mm(mid, ffn_wd_b, (j,), d, F32, res=x, tk=(d_ff + f_pad) // 4, name="ffn_down")
        else:
            h2, gates = _norm_router(x, norm_ffn[l], router_b[j], n_experts)
            mid = _glu_up(h2, moe_wg_b, moe_wu_b, (j,), n_experts, f_exp, gates=gates, tm=2048)
            x = _mm(mid, moe_wd_b, (j,), d, F32, res=x, tk=n_experts * f_exp // 4, name="moe_down")
    outs = [_rmsnorm(x, norm_final, F32, row0, nb * seq_len) for row0, nb, seq_len in seqs]
    return outs[0].reshape(x_prompt.shape), outs[1].reshape(x_sample.shape)
```
